```python
import jax, jax.numpy as jnp
from jax import lax
import numpy as np

D_MODEL = 1024
BATCH = 8
SEQ = 2048
DEPTH = 2

HEAD_DIM = 64
N_MIXERS = 4
N_HEADS = D_MODEL // (N_MIXERS * HEAD_DIM)
W_A = N_HEADS * HEAD_DIM
W_B = N_HEADS * HEAD_DIM
W_C = N_HEADS * HEAD_DIM
W_D = N_HEADS * HEAD_DIM
D_MIX = W_A + W_B + W_C + W_D
IN_COLS = 3 * W_A + 2 * W_B + 3 * W_C + 3 * W_D
MOBA_BLOCK = 256
MOBA_TOPK = 3
MOBA_Q_CHUNK = 32
SGU_CHUNK = 128
CONV_WIDTH = 3
DILATED_PATTERNS = ((128, 1), (512, 4), (2048, 16))
D_FF = 2816
N_EXPERTS = 8
TOP_K = 2
MOE_BLOCK = 256
N_DENSE = (DEPTH + 1) // 2
N_MOE = DEPTH // 2
RMS_EPS = 1e-6
LN_EPS = 1e-5
ATTN_SCALE = HEAD_DIM ** -0.5

kernel_name = 'hybrid_moba_sgu_conv_dilated_moe_block'


def rms_norm(x, g):
    xf = x.astype(jnp.float32)
    y = xf * lax.rsqrt(jnp.mean(xf * xf, axis=-1, keepdims=True) + RMS_EPS)
    return (y * g.astype(jnp.float32)).astype(x.dtype)


def layer_norm(x, g):
    xf = x.astype(jnp.float32)
    mu = jnp.mean(xf, axis=-1, keepdims=True)
    xc = xf - mu
    y = xc * lax.rsqrt(jnp.mean(xc * xc, axis=-1, keepdims=True) + LN_EPS)
    return (y * g.astype(jnp.float32)).astype(x.dtype)


def moba_attention(q, k, v):
    b, h, s, dh = q.shape
    nb = -(-s // MOBA_BLOCK)
    pad = nb * MOBA_BLOCK - s
    padw = ((0, 0), (0, 0), (0, pad), (0, 0))
    kb = jnp.pad(k, padw).reshape(b, h, nb, MOBA_BLOCK, dh)
    vb = jnp.pad(v, padw).reshape(b, h, nb, MOBA_BLOCK, dh)
    k_mean = jnp.mean(kb.astype(jnp.float32), axis=3)
    pos = jnp.arange(s)
    q_blk = pos // MOBA_BLOCK
    gate = jnp.einsum('bhsd,bhnd->bhsn', q.astype(jnp.float32), k_mean)
    past = jnp.arange(nb)[None, :] < q_blk[:, None]
    gate = jnp.where(past, gate, -jnp.inf)
    n_sel = min(MOBA_TOPK, nb)
    _, sel = lax.top_k(gate, n_sel)
    own = jnp.broadcast_to(q_blk[:, None], (b, h, s, 1)).astype(sel.dtype)
    idx = jnp.concatenate([sel, own], axis=-1)
    valid = jnp.concatenate([jnp.arange(n_sel)[None, :] < q_blk[:, None],
                             jnp.ones((s, 1), dtype=bool)], axis=-1)
    n = n_sel + 1
    qc = MOBA_Q_CHUNK
    nc = s // qc
    q_c = q.reshape(b, h, nc, qc, dh).transpose(2, 0, 1, 3, 4)
    idx_c = idx.reshape(b, h, nc, qc, n).transpose(2, 0, 1, 3, 4)
    valid_c = valid.reshape(nc, qc, n)
    pos_c = pos.reshape(nc, qc)
    bi = jnp.arange(b)[:, None, None, None]
    hi = jnp.arange(h)[None, :, None, None]
    offs = jnp.arange(MOBA_BLOCK)

    def attend_chunk(args):
        qq, ii, vv, tt = args
        kg = kb[bi, hi, ii]
        vg = vb[bi, hi, ii]
        kpos = ii[..., None] * MOBA_BLOCK + offs
        mask = vv[None, None, :, :, None] & (kpos <= tt[None, None, :, None, None])
        sc = jnp.einsum('bhqd,bhqnkd->bhqnk', qq, kg).astype(jnp.float32) * ATTN_SCALE
        sc = jnp.where(mask, sc, -jnp.inf).reshape(b, h, qc, n * MOBA_BLOCK)
        p = jax.nn.softmax(sc, axis=-1).reshape(b, h, qc, n, MOBA_BLOCK).astype(v.dtype)
        return jnp.einsum('bhqnk,bhqnkd->bhqd', p, vg)

    out = lax.map(attend_chunk, (q_c, idx_c, valid_c, pos_c))
    return out.transpose(1, 2, 0, 3, 4).reshape(b, h, s, dh)


def banded_causal_attention(q, k, v, reach):
    *lead, m, dh = q.shape
    blk = reach
    nblk = -(-m // blk)
    pad = nblk * blk - m
    padw = [(0, 0)] * len(lead) + [(0, pad), (0, 0)]

    def blocks(t):
        return jnp.pad(t, padw).reshape(*lead, nblk, blk, dh)

    def with_prev(t):
        prev = jnp.concatenate([jnp.zeros_like(t[..., :1, :, :]), t[..., :-1, :, :]], axis=-3)
        return jnp.concatenate([prev, t], axis=-2)

    qb = blocks(q)
    kw = with_prev(blocks(k))
    vw = with_prev(blocks(v))
    qi = jnp.arange(blk)[:, None] + blk
    kj = jnp.arange(2 * blk)[None, :]
    rel = qi - kj
    band = (rel >= 0) & (rel <= reach)
    has_prev = (jnp.arange(nblk) > 0)[:, None, None] | (kj >= blk)[None]
    mask = band[None] & has_prev
    sc = jnp.einsum('...nqd,...nkd->...nqk', qb, kw).astype(jnp.float32) * ATTN_SCALE
    sc = jnp.where(mask, sc, -jnp.inf)
    lse = jax.nn.logsumexp(sc, axis=-1)
    p = jnp.exp(sc - lse[..., None]).astype(v.dtype)
    o = jnp.einsum('...nqk,...nkd->...nqd', p, vw)
    o = o.reshape(*lead, nblk * blk, dh)[..., :m, :]
    lse = lse.reshape(*lead, nblk * blk)[..., :m]
    return o, lse


def dilated_mixture_attention(q, k, v):
    b, h, s, dh = q.shape
    outs, lses = [], []
    for window, dil in DILATED_PATTERNS:
        m = s // dil
        qs = q.reshape(b, h, m, dil, dh).transpose(0, 1, 3, 2, 4)
        ks = k.reshape(b, h, m, dil, dh).transpose(0, 1, 3, 2, 4)
        vs = v.reshape(b, h, m, dil, dh).transpose(0, 1, 3, 2, 4)
        o, lse = banded_causal_attention(qs, ks, vs, window // dil)
        outs.append(o.transpose(0, 1, 3, 2, 4).reshape(b, h, s, dh))
        lses.append(lse.transpose(0, 1, 3, 2).reshape(b, h, s))
    wts = jax.nn.softmax(jnp.stack(lses, axis=0), axis=0)
    return jnp.einsum('pbhs,pbhsd->bhsd', wts.astype(q.dtype), jnp.stack(outs, axis=0))


def spatial_gating(z_b, ln_g, w_s, b_s):
    b, s, _ = z_b.shape
    u, v = jnp.split(jax.nn.gelu(z_b), 2, axis=-1)
    v = layer_norm(v, ln_g)
    v = v.reshape(b, s // SGU_CHUNK, SGU_CHUNK, N_HEADS, W_B // N_HEADS)
    causal = jnp.tril(jnp.ones((SGU_CHUNK, SGU_CHUNK), w_s.dtype))
    sv = jnp.einsum('gts,bnsgc->bntgc', w_s * causal, v) + b_s.T[:, :, None]
    return u * sv.reshape(b, s, W_B)


def short_conv_mixer(z_c, conv_w):
    s = z_c.shape[1]
    bg, cg, xc = jnp.split(z_c, 3, axis=-1)
    z = cg * xc
    zp = jnp.pad(z, ((0, 0), (CONV_WIDTH - 1, 0), (0, 0)))
    y = zp[:, 0:s] * conv_w[0]
    for tap in range(1, CONV_WIDTH):
        y = y + zp[:, tap:tap + s] * conv_w[tap]
    return bg * y


def hybrid_mixer(h, w_in, sgu_ln_g, sgu_w, sgu_b, conv_w, w_out):
    b, s, _ = h.shape
    z = h @ w_in
    cut1 = 3 * W_A
    cut2 = cut1 + 2 * W_B
    cut3 = cut2 + 3 * W_C
    z_a, z_b, z_c, z_d = jnp.split(z, [cut1, cut2, cut3], axis=-1)

    def heads(t):
        return t.reshape(b, s, 3, N_HEADS, HEAD_DIM).transpose(2, 0, 3, 1, 4)

    qa, ka, va = heads(z_a)
    y_a = moba_attention(qa, ka, va).transpose(0, 2, 1, 3).reshape(b, s, W_A)
    y_b = spatial_gating(z_b, sgu_ln_g, sgu_w, sgu_b)
    y_c = short_conv_mixer(z_c, conv_w)
    qd, kd, vd = heads(z_d)
    y_d = dilated_mixture_attention(qd, kd, vd).transpose(0, 2, 1, 3).reshape(b, s, W_D)
    return jnp.concatenate([y_a, y_b, y_c, y_d], axis=-1) @ w_out


def swiglu(h, w_gate, w_up, w_down):
    return (jax.nn.silu(h @ w_gate) * (h @ w_up)) @ w_down


def moe_swiglu(h, w_router, w_gate, w_up, w_down):
    b, s, d = h.shape
    x = h.reshape(-1, d)
    n = x.shape[0]
    logits = (x @ w_router).astype(jnp.float32)
    top_val, top_idx = lax.top_k(logits, TOP_K)
    gates = jax.nn.softmax(top_val, axis=-1)
    flat_e = top_idx.reshape(-1)
    flat_tok = jnp.arange(n * TOP_K) // TOP_K
    flat_g = gates.reshape(-1)
    order = jnp.argsort(flat_e)
    se = flat_e[order]
    stok = flat_tok[order]
    sg = flat_g[order]
    counts = jnp.bincount(flat_e, length=N_EXPERTS)
    starts = jnp.cumsum(counts) - counts
    padded = ((counts + MOE_BLOCK - 1) // MOE_BLOCK) * MOE_BLOCK
    pends = jnp.cumsum(padded)
    pstarts = pends - padded
    rank = jnp.arange(n * TOP_K) - starts[se]
    dest = pstarts[se] + rank
    nblocks = -(-(n * TOP_K) // MOE_BLOCK) + N_EXPERTS
    cap = nblocks * MOE_BLOCK
    xbuf = jnp.zeros((cap, d), x.dtype).at[dest].set(x[stok])
    block_e = jnp.minimum(jnp.searchsorted(pends, jnp.arange(nblocks) * MOE_BLOCK, side='right'),
                          N_EXPERTS - 1)

    def expert_block(args):
        xb, e = args
        return (jax.nn.silu(xb @ w_gate[e]) * (xb @ w_up[e])) @ w_down[e]

    ybuf = lax.map(expert_block, (xbuf.reshape(nblocks, MOE_BLOCK, d), block_e)).reshape(cap, d)
    y = ybuf[dest] * sg[:, None].astype(x.dtype)
    out = jax.ops.segment_sum(y, stok, num_segments=n)
    return out.reshape(b, s, d)


def setup_inputs(seed: int = 0) -> dict:
    key = jax.random.key(seed)
    ks = jax.random.split(key, 18)

    def nrm(k, shape, scale):
        return jax.random.normal(k, shape, jnp.float32) * scale

    return {
        'x': nrm(ks[0], (BATCH, SEQ, D_MODEL), 1.0),
        'g_mix': 1.0 + nrm(ks[1], (DEPTH, D_MODEL), 0.02),
        'w_in': nrm(ks[2], (DEPTH, D_MODEL, IN_COLS), D_MODEL ** -0.5),
        'sgu_ln_g': 1.0 + nrm(ks[3], (DEPTH, W_B), 0.02),
        'sgu_w': nrm(ks[4], (DEPTH, N_HEADS, SGU_CHUNK, SGU_CHUNK), SGU_CHUNK ** -0.5),
        'sgu_b': nrm(ks[5], (DEPTH, N_HEADS, SGU_CHUNK), 0.02),
        'conv_w': nrm(ks[6], (DEPTH, CONV_WIDTH, W_C), CONV_WIDTH ** -0.5),
        'w_out': nrm(ks[7], (DEPTH, D_MIX, D_MODEL), D_MIX ** -0.5),
        'g_ffn': 1.0 + nrm(ks[8], (DEPTH, D_MODEL), 0.02),
        'dense_w_gate': nrm(ks[9], (N_DENSE, D_MODEL, D_FF), D_MODEL ** -0.5),
        'dense_w_up': nrm(ks[10], (N_DENSE, D_MODEL, D_FF), D_MODEL ** -0.5),
        'dense_w_down': nrm(ks[11], (N_DENSE, D_FF, D_MODEL), D_FF ** -0.5),
        'router_w': nrm(ks[12], (N_MOE, D_MODEL, N_EXPERTS), D_MODEL ** -0.5),
        'moe_w_gate': nrm(ks[13], (N_MOE, N_EXPERTS, D_MODEL, D_FF), D_MODEL ** -0.5),
        'moe_w_up': nrm(ks[14], (N_MOE, N_EXPERTS, D_MODEL, D_FF), D_MODEL ** -0.5),
        'moe_w_down': nrm(ks[15], (N_MOE, N_EXPERTS, D_FF, D_MODEL), D_FF ** -0.5),
        'g_final': 1.0 + nrm(ks[16], (D_MODEL,), 0.02),
    }


def reference(x, g_mix, w_in, sgu_ln_g, sgu_w, sgu_b, conv_w, w_out, g_ffn,
              dense_w_gate, dense_w_up, dense_w_down, router_w,
              moe_w_gate, moe_w_up, moe_w_down, g_final):
    for l in range(DEPTH):
        h = rms_norm(x, g_mix[l])
        x = x + hybrid_mixer(h, w_in[l], sgu_ln_g[l], sgu_w[l], sgu_b[l], conv_w[l], w_out[l])
        h = rms_norm(x, g_ffn[l])
        j = l // 2
        if l % 2 == 0:
            x = x + swiglu(h, dense_w_gate[j], dense_w_up[j], dense_w_down[j])
        else:
            x = x + moe_swiglu(h, router_w[j], moe_w_gate[j], moe_w_up[j], moe_w_down[j])
    return rms_norm(x, g_final)
```

```python
import functools

import numpy as np
import jax
import jax.numpy as jnp
from jax import lax
from jax.experimental import pallas as pl
from jax.experimental.pallas import tpu as pltpu

D_MODEL = 1024
BATCH = 8
SEQ = 2048
DEPTH = 2
N_TOK = BATCH * SEQ
HEAD_DIM = 64
N_HEADS = 4
W_MIX = N_HEADS * HEAD_DIM
CUT_A = 3 * W_MIX
CUT_B = CUT_A + 2 * W_MIX
CUT_C = CUT_B + 3 * W_MIX
IN_COLS = CUT_C + 3 * W_MIX
ATT_BLOCK = 256
N_ATT_BLOCKS = SEQ // ATT_BLOCK
MOBA_TOPK = 3
SGU_CHUNK = 128
DILATED_PATTERNS = ((128, 1), (512, 4), (2048, 16))
D_FF = 2816
FF_CHUNK = D_FF // 2
N_EXPERTS = 8
MOE_BLOCK = 256
N_MOE_BLOCKS = (N_TOK * 2) // MOE_BLOCK + N_EXPERTS
MOE_CAP = N_MOE_BLOCKS * MOE_BLOCK
RMS_EPS = 1e-6
LN_EPS = 1e-5
ATTN_SCALE = HEAD_DIM ** -0.5
MASK_BIAS = -1e30

LANES = 128
ROW_TILE = 512
VMEM_LIMIT = 56 * 1024 * 1024

F32 = jnp.float32
BF16 = jnp.bfloat16


def _dot(a, b):
    return jnp.dot(a, b, preferred_element_type=F32)


def _dot_nt(a, b):
    return lax.dot_general(a, b, (((1,), (1,)), ((), ())), preferred_element_type=F32)


def _rms(x, g):
    return x * lax.rsqrt(jnp.mean(x * x, axis=-1, keepdims=True) + RMS_EPS) * g


def _split_bf16(x):
    hi = x.astype(BF16)
    lo = (x - hi.astype(F32)).astype(BF16)
    return hi, lo


def _norm_inproj_kernel(x_ref, g_ref, w_ref, za_ref, zb_ref, zc_ref, zd_ref):
    h = _rms(x_ref[...], g_ref[...]).astype(BF16)
    za_ref[...] = _dot(h, w_ref[:, 0:CUT_A]).astype(BF16)
    zb_ref[...] = _dot(h, w_ref[:, CUT_A:CUT_B]).astype(BF16)
    zc_ref[...] = _dot(h, w_ref[:, CUT_B:CUT_C]).astype(BF16)
    zd_ref[...] = _dot(h, w_ref[:, CUT_C:IN_COLS]).astype(BF16)


def _norm_inproj(x, g, w):
    widths = (CUT_A, CUT_B - CUT_A, CUT_C - CUT_B, IN_COLS - CUT_C)
    return pl.pallas_call(
        _norm_inproj_kernel,
        grid=(N_TOK // ROW_TILE,),
        in_specs=[
            pl.BlockSpec((ROW_TILE, D_MODEL), lambda i: (i, 0)),
            pl.BlockSpec((1, D_MODEL), lambda i: (0, 0)),
            pl.BlockSpec((D_MODEL, IN_COLS), lambda i: (0, 0)),
        ],
        out_specs=[pl.BlockSpec((ROW_TILE, w_), lambda i: (i, 0)) for w_ in widths],
        out_shape=[jax.ShapeDtypeStruct((N_TOK, w_), BF16) for w_ in widths],
        compiler_params=pltpu.CompilerParams(
            dimension_semantics=("parallel",), vmem_limit_bytes=VMEM_LIMIT),
        name="norm_inproj",
    )(x, g, w)


def _head_lane_mask(h, width):
    lane = lax.broadcasted_iota(jnp.int32, (ATT_BLOCK, width), 1)
    lo = (h * HEAD_DIM) % width
    return (lane >= lo) & (lane < lo + HEAD_DIM)


def _store_masked_v(z_ref, vm_ref):
    for h in range(N_HEADS):
        t = h // 2
        keep = _head_lane_mask(h % 2, LANES)
        for j in range(N_ATT_BLOCKS):
            rows = slice(j * ATT_BLOCK, (j + 1) * ATT_BLOCK)
            v = z_ref[rows, 2 * W_MIX + t * LANES:2 * W_MIX + (t + 1) * LANES].astype(F32)
            vm_ref[h, rows, :] = jnp.where(keep, v, 0.0).astype(BF16)


def _softmax_step(s, vm, state):
    m, l, acc = state
    m_new = jnp.maximum(m, jnp.max(s, axis=-1, keepdims=True))
    alpha = jnp.exp(m - m_new)
    p = jnp.exp(s - m_new)
    l = alpha * l + jnp.sum(p, axis=-1, keepdims=True)
    acc = alpha * acc + _dot(p.astype(BF16), vm)
    return m_new, l, acc


def _softmax_first(s, vm):
    m = jnp.max(s, axis=-1, keepdims=True)
    p = jnp.exp(s - m)
    return m, jnp.sum(p, axis=-1, keepdims=True), _dot(p.astype(BF16), vm)


def _write_heads(o_ref, states):
    for t in range(2):
        (_, l0, a0), (_, l1, a1) = states[2 * t], states[2 * t + 1]
        even = _head_lane_mask(0, LANES)
        o_ref[:, t * LANES:(t + 1) * LANES] = jnp.where(even, a0 / l0, a1 / l1).astype(o_ref.dtype)


_GATE_LANE = (64, 0, 8, 16)


def _moba_kernel(z_ref, o_ref, kmt_hi_ref, kmt_lo_ref, kaug_ref, vm_ref):
    i = pl.program_id(1)

    @pl.when(i == 0)
    def _prepare_batch():
        means = []
        for j in range(N_ATT_BLOCKS):
            kj = z_ref[j * ATT_BLOCK:(j + 1) * ATT_BLOCK, W_MIX:2 * W_MIX].astype(F32)
            means.append(jnp.sum(kj, axis=0, keepdims=True) * (1.0 / ATT_BLOCK))
        m8 = jnp.concatenate(means, axis=0)
        mt = jnp.concatenate([m8] * (LANES // N_ATT_BLOCKS), axis=0)
        r = lax.broadcasted_iota(jnp.int32, (LANES, W_MIX), 0)
        c = lax.broadcasted_iota(jnp.int32, (LANES, W_MIX), 1)
        head_of_row = jnp.where(r < 24, (r >> 3) + 1, jnp.where((r >= 64) & (r < 72), 0, -1))
        kmt = jnp.where((c >> 6) == head_of_row, mt, 0.0)
        hi, lo = _split_bf16(kmt)
        kmt_hi_ref[...] = hi
        kmt_lo_ref[...] = lo
        lane = lax.broadcasted_iota(jnp.int32, (ATT_BLOCK, LANES), 1)
        for h in range(N_HEADS):
            g0 = _GATE_LANE[h]
            in_gate = (lane >= g0) & (lane < g0 + N_ATT_BLOCKS)
            for j in range(N_ATT_BLOCKS):
                rows = slice(j * ATT_BLOCK, (j + 1) * ATT_BLOCK)
                k0 = z_ref[rows, W_MIX:W_MIX + LANES].astype(F32)
                onehot = jnp.where(lane == g0 + j, 1.0, 0.0)
                kaug_ref[h, rows, :] = jnp.where(in_gate, onehot, k0).astype(BF16)
        _store_masked_v(z_ref, vm_ref)

    row0 = pl.multiple_of(i * ATT_BLOCK, ATT_BLOCK)
    q = z_ref[pl.ds(row0, ATT_BLOCK), 0:W_MIX]
    qf = q.astype(F32)

    gates = _dot_nt(q, kmt_hi_ref[...]) + _dot_nt(q, kmt_lo_ref[...])
    lane = lax.broadcasted_iota(jnp.int32, (ATT_BLOCK, LANES), 1)
    blk = lane & 7
    is_gate = (lane < 24) | ((lane >= 64) & (lane < 72))
    past = is_gate & (blk < i)
    g = jnp.where(past, gates, -jnp.inf)
    rank = jnp.zeros((ATT_BLOCK, LANES), F32)
    for d in range(1, N_ATT_BLOCKS):
        lower = pltpu.roll(g, d, axis=1)
        rank = rank + jnp.where((blk >= d) & (lower >= g), 1.0, 0.0)
        upper = pltpu.roll(g, LANES - d, axis=1)
        rank = rank + jnp.where((blk + d < N_ATT_BLOCKS) & (upper > g), 1.0, 0.0)
    keep = (past & (rank < MOBA_TOPK)) | (blk == i)
    bias = jnp.where(is_gate & jnp.logical_not(keep), MASK_BIAS, 0.0)

    qs = qf * ATTN_SCALE
    q_aug = []
    for h in range(N_HEADS):
        g0 = _GATE_LANE[h]
        in_gate = (lane >= g0) & (lane < g0 + N_ATT_BLOCKS)
        t0 = jnp.where(in_gate, bias, 0.0)
        if h < 2:
            t0 = jnp.where(_head_lane_mask(h, LANES), qs[:, 0:LANES], t0)
            t1 = jnp.zeros((ATT_BLOCK, LANES), F32)
        else:
            t1 = jnp.where(_head_lane_mask(h - 2, LANES), qs[:, LANES:2 * LANES], 0.0)
        q_aug.append(jnp.concatenate([t0, t1], axis=1).astype(BF16))

    def scores(h, r0):
        k = jnp.concatenate([kaug_ref[h, pl.ds(r0, ATT_BLOCK), :],
                             z_ref[pl.ds(r0, ATT_BLOCK), W_MIX + LANES:2 * W_MIX]], axis=1)
        return _dot_nt(q_aug[h], k)

    qi = lax.broadcasted_iota(jnp.int32, (ATT_BLOCK, ATT_BLOCK), 0)
    ki = lax.broadcasted_iota(jnp.int32, (ATT_BLOCK, ATT_BLOCK), 1)
    states = []
    for h in range(N_HEADS):
        s = jnp.where(ki <= qi, scores(h, row0), -jnp.inf)
        states.append(_softmax_first(s, vm_ref[h, pl.ds(row0, ATT_BLOCK), :]))

    def past_block(j, carry):
        r0 = pl.multiple_of(j * ATT_BLOCK, ATT_BLOCK)
        return tuple(_softmax_step(scores(h, r0), vm_ref[h, pl.ds(r0, ATT_BLOCK), :], carry[h])
                     for h in range(N_HEADS))

    states = lax.fori_loop(0, i, past_block, tuple(states))
    _write_heads(o_ref, states)


def _moba(za):
    za3 = za.reshape(BATCH, SEQ, CUT_A)
    out = pl.pallas_call(
        _moba_kernel,
        grid=(BATCH, N_ATT_BLOCKS),
        in_specs=[pl.BlockSpec((None, SEQ, CUT_A), lambda b, i: (b, 0, 0))],
        out_specs=pl.BlockSpec((None, ATT_BLOCK, W_MIX), lambda b, i: (b, i, 0)),
        out_shape=jax.ShapeDtypeStruct((BATCH, SEQ, W_MIX), BF16),
        scratch_shapes=[
            pltpu.VMEM((LANES, W_MIX), BF16),
            pltpu.VMEM((LANES, W_MIX), BF16),
            pltpu.VMEM((N_HEADS, SEQ, LANES), BF16),
            pltpu.VMEM((N_HEADS, SEQ, LANES), BF16),
        ],
        compiler_params=pltpu.CompilerParams(
            dimension_semantics=("parallel", "arbitrary"), vmem_limit_bytes=VMEM_LIMIT),
        name="moba_attention",
    )(za3)
    return out.reshape(N_TOK, W_MIX)


def _dilated_bias_table():
    d = np.arange(ATT_BLOCK)[:, None] - np.arange(ATT_BLOCK)[None, :]
    tiles = []
    for k in range(N_ATT_BLOCKS):
        dist = d + k * ATT_BLOCK
        count = np.zeros(dist.shape, np.float64)
        for window, dil in DILATED_PATTERNS:
            count += (dist >= 0) & (dist <= window) & (dist % dil == 0)
        with np.errstate(divide="ignore"):
            tiles.append(np.log(count))
    return jnp.asarray(np.stack(tiles), F32)


def _dilated_kernel(z_ref, bias_ref, o_ref, vm_ref):
    i = pl.program_id(1)

    @pl.when(i == 0)
    def _prepare_batch():
        _store_masked_v(z_ref, vm_ref)

    row0 = pl.multiple_of(i * ATT_BLOCK, ATT_BLOCK)
    qs = z_ref[pl.ds(row0, ATT_BLOCK), 0:W_MIX].astype(F32) * ATTN_SCALE
    q_stack = jnp.concatenate(
        [jnp.where(_head_lane_mask(h, W_MIX), qs, 0.0) for h in range(N_HEADS)], axis=0).astype(BF16)

    def scores(r0, dist_blocks):
        k = z_ref[pl.ds(r0, ATT_BLOCK), W_MIX:2 * W_MIX]
        s = _dot_nt(q_stack, k)
        b = bias_ref[dist_blocks]
        return [s[h * ATT_BLOCK:(h + 1) * ATT_BLOCK] + b for h in range(N_HEADS)]

    s0 = scores(row0, 0)
    states = tuple(_softmax_first(s0[h], vm_ref[h, pl.ds(row0, ATT_BLOCK), :])
                   for h in range(N_HEADS))

    def past_block(j, carry):
        r0 = pl.multiple_of(j * ATT_BLOCK, ATT_BLOCK)
        s = scores(r0, i - j)
        return tuple(_softmax_step(s[h], vm_ref[h, pl.ds(r0, ATT_BLOCK), :], carry[h])
                     for h in range(N_HEADS))

    states = lax.fori_loop(0, i, past_block, states)
    _write_heads(o_ref, states)


def _dilated(zd, bias_table):
    zd3 = zd.reshape(BATCH, SEQ, 3 * W_MIX)
    out = pl.pallas_call(
        _dilated_kernel,
        grid=(BATCH, N_ATT_BLOCKS),
        in_specs=[
            pl.BlockSpec((None, SEQ, 3 * W_MIX), lambda b, i: (b, 0, 0)),
            pl.BlockSpec((N_ATT_BLOCKS, ATT_BLOCK, ATT_BLOCK), lambda b, i: (0, 0, 0)),
        ],
        out_specs=pl.BlockSpec((None, ATT_BLOCK, W_MIX), lambda b, i: (b, i, 0)),
        out_shape=jax.ShapeDtypeStruct((BATCH, SEQ, W_MIX), BF16),
        scratch_shapes=[pltpu.VMEM((N_HEADS, SEQ, LANES), BF16)],
        compiler_params=pltpu.CompilerParams(
            dimension_semantics=("parallel", "arbitrary"), vmem_limit_bytes=VMEM_LIMIT),
        name="dilated_attention",
    )(zd3, bias_table)
    return out.reshape(N_TOK, W_MIX)


def _gelu_tanh(x):
    return 0.5 * x * (1.0 + jnp.tanh(np.sqrt(2.0 / np.pi).astype(np.float32) * (x + 0.044715 * (x * x * x))))


def _sgu_conv_kernel(zb_ref, zc_ref, lng_ref, ws_ref, bs_ref, cw_ref, yb_ref, yc_ref):
    r = lax.broadcasted_iota(jnp.int32, (N_HEADS * SGU_CHUNK, SGU_CHUNK), 0)
    c = lax.broadcasted_iota(jnp.int32, (N_HEADS * SGU_CHUNK, SGU_CHUNK), 1)
    w_stack = jnp.where((r & (SGU_CHUNK - 1)) >= c, ws_ref[...], 0.0).astype(BF16)
    group = lax.broadcasted_iota(jnp.int32, (SGU_CHUNK, W_MIX), 1) >> 6
    ln_g = lng_ref[...]
    b_tile = bs_ref[...]

    def chunk(n, carry):
        r0 = pl.multiple_of(n * SGU_CHUNK, SGU_CHUNK)
        gz = _gelu_tanh(zb_ref[pl.ds(r0, SGU_CHUNK), :].astype(F32))
        u = gz[:, 0:W_MIX]
        v = gz[:, W_MIX:2 * W_MIX]
        vc = v - jnp.mean(v, axis=-1, keepdims=True)
        vn = vc * lax.rsqrt(jnp.mean(vc * vc, axis=-1, keepdims=True) + LN_EPS) * ln_g
        mixed = _dot(w_stack, vn.astype(BF16))
        sv = mixed[0:SGU_CHUNK]
        for gi in range(1, N_HEADS):
            sv = jnp.where(group == gi, mixed[gi * SGU_CHUNK:(gi + 1) * SGU_CHUNK], sv)
        yb_ref[pl.ds(r0, SGU_CHUNK), :] = (u * (sv + b_tile)).astype(yb_ref.dtype)
        return carry

    lax.fori_loop(0, SEQ // SGU_CHUNK, chunk, 0)

    zc = zc_ref[...].astype(F32)
    z = zc[:, W_MIX:2 * W_MIX] * zc[:, 2 * W_MIX:3 * W_MIX]
    t = lax.broadcasted_iota(jnp.int32, (SEQ, W_MIX), 0)
    z1 = jnp.where(t >= 1, pltpu.roll(z, 1, axis=0), 0.0)
    z2 = jnp.where(t >= 2, pltpu.roll(z, 2, axis=0), 0.0)
    y = z2 * cw_ref[0:1, :]
    y = y + z1 * cw_ref[1:2, :]
    y = y + z * cw_ref[2:3, :]
    yc_ref[...] = (zc[:, 0:W_MIX] * y).astype(yc_ref.dtype)


def _sgu_conv(zb, zc, ln_g, w_s, b_s, conv_w):
    w_stack = w_s.reshape(N_HEADS * SGU_CHUNK, SGU_CHUNK)
    b_tile = jnp.repeat(b_s.T, HEAD_DIM, axis=1)
    yb, yc = pl.pallas_call(
        _sgu_conv_kernel,
        grid=(BATCH,),
        in_specs=[
            pl.BlockSpec((None, SEQ, 2 * W_MIX), lambda b: (b, 0, 0)),
            pl.BlockSpec((None, SEQ, 3 * W_MIX), lambda b: (b, 0, 0)),
            pl.BlockSpec((1, W_MIX), lambda b: (0, 0)),
            pl.BlockSpec((N_HEADS * SGU_CHUNK, SGU_CHUNK), lambda b: (0, 0)),
            pl.BlockSpec((SGU_CHUNK, W_MIX), lambda b: (0, 0)),
            pl.BlockSpec((3, W_MIX), lambda b: (0, 0)),
        ],
        out_specs=[pl.BlockSpec((None, SEQ, W_MIX), lambda b: (b, 0, 0))] * 2,
        out_shape=[jax.ShapeDtypeStruct((BATCH, SEQ, W_MIX), BF16)] * 2,
        compiler_params=pltpu.CompilerParams(
            dimension_semantics=("parallel",), vmem_limit_bytes=VMEM_LIMIT),
        name="sgu_conv",
    )(zb.reshape(BATCH, SEQ, 2 * W_MIX), zc.reshape(BATCH, SEQ, 3 * W_MIX),
      ln_g.reshape(1, W_MIX), w_stack, b_tile, conv_w)
    return yb.reshape(N_TOK, W_MIX), yc.reshape(N_TOK, W_MIX)


def _outproj_residual(x_ref, ya_ref, yb_ref, yc_ref, yd_ref, w_ref):
    acc = _dot(ya_ref[...], w_ref[0:W_MIX, :])
    acc = acc + _dot(yb_ref[...], w_ref[W_MIX:2 * W_MIX, :])
    acc = acc + _dot(yc_ref[...], w_ref[2 * W_MIX:3 * W_MIX, :])
    acc = acc + _dot(yd_ref[...], w_ref[3 * W_MIX:4 * W_MIX, :])
    return x_ref[...] + acc


def _outproj_dense_kernel(x_ref, ya_ref, yb_ref, yc_ref, yd_ref, w_ref, g_ref, x1_ref, h_ref):
    x1 = _outproj_residual(x_ref, ya_ref, yb_ref, yc_ref, yd_ref, w_ref)
    x1_ref[...] = x1
    h_ref[...] = _rms(x1, g_ref[...]).astype(BF16)


_R_E1, _R_E2, _R_G1, _R_G2, _R_RANK1, _R_RANK2 = range(6)


def _outproj_router_kernel(x_ref, ya_ref, yb_ref, yc_ref, yd_ref, w_ref, g_ref, wr_hi_ref, wr_lo_ref,
                           x1_ref, h_ref, route_ref, counts_ref, running_ref):
    step = pl.program_id(0)

    @pl.when(step == 0)
    def _():
        running_ref[...] = jnp.zeros_like(running_ref)

    x1 = _outproj_residual(x_ref, ya_ref, yb_ref, yc_ref, yd_ref, w_ref)
    x1_ref[...] = x1
    h = _rms(x1, g_ref[...])
    h_ref[...] = h

    h_hi, h_lo = _split_bf16(h)
    logits = _dot(h_hi, wr_hi_ref[...]) + (_dot(h_lo, wr_hi_ref[...]) + _dot(h_hi, wr_lo_ref[...]))
    lane = lax.broadcasted_iota(jnp.int32, (ROW_TILE, LANES), 1)
    lane_f = lane.astype(F32)
    logits = jnp.where(lane < N_EXPERTS, logits, -jnp.inf)
    m1 = jnp.max(logits, axis=-1, keepdims=True)
    e1 = jnp.min(jnp.where(logits == m1, lane_f, float(LANES)), axis=-1, keepdims=True)
    rest = jnp.where(lane_f == e1, -jnp.inf, logits)
    m2 = jnp.max(rest, axis=-1, keepdims=True)
    e2 = jnp.min(jnp.where(rest == m2, lane_f, float(LANES)), axis=-1, keepdims=True)
    t = jnp.exp(m2 - m1)
    g1 = 1.0 / (1.0 + t)
    g2 = t / (1.0 + t)

    chosen = jnp.where((lane_f == e1) | (lane_f == e2), 1.0, 0.0)
    ri = lax.broadcasted_iota(jnp.int32, (ROW_TILE, ROW_TILE), 0)
    ci = lax.broadcasted_iota(jnp.int32, (ROW_TILE, ROW_TILE), 1)
    earlier = jnp.where(ci < ri, 1.0, 0.0).astype(BF16)
    before = _dot(earlier, chosen.astype(BF16)) + running_ref[...]
    rank1 = jnp.sum(jnp.where(lane_f == e1, before, 0.0), axis=-1, keepdims=True)
    rank2 = jnp.sum(jnp.where(lane_f == e2, before, 0.0), axis=-1, keepdims=True)
    running_ref[...] = running_ref[...] + jnp.sum(chosen, axis=0, keepdims=True)
    counts_ref[...] = running_ref[...]

    rec = jnp.zeros((ROW_TILE, LANES), F32)
    for pos, val in ((_R_E1, e1), (_R_E2, e2), (_R_G1, g1), (_R_G2, g2),
                     (_R_RANK1, rank1), (_R_RANK2, rank2)):
        rec = jnp.where(lane == pos, val, rec)
    route_ref[...] = rec


def _outproj(x, ys, w, g, router=None):
    row = lambda width: pl.BlockSpec((ROW_TILE, width), lambda i: (i, 0))
    const = lambda shape: pl.BlockSpec(shape, lambda i: (0, 0))
    in_specs = [row(D_MODEL)] + [row(W_MIX)] * 4 + [const((D_MODEL, D_MODEL)), const((1, D_MODEL))]
    if router is None:
        return pl.pallas_call(
            _outproj_dense_kernel,
            grid=(N_TOK // ROW_TILE,),
            in_specs=in_specs,
            out_specs=[row(D_MODEL), row(D_MODEL)],
            out_shape=[jax.ShapeDtypeStruct((N_TOK, D_MODEL), F32),
                       jax.ShapeDtypeStruct((N_TOK, D_MODEL), BF16)],
            compiler_params=pltpu.CompilerParams(
                dimension_semantics=("parallel",), vmem_limit_bytes=VMEM_LIMIT),
            name="outproj_norm",
        )(x, *ys, w, g)
    wr_hi, wr_lo = router
    return pl.pallas_call(
        _outproj_router_kernel,
        grid=(N_TOK // ROW_TILE,),
        in_specs=in_specs + [const((D_MODEL, LANES)), const((D_MODEL, LANES))],
        out_specs=[row(D_MODEL), row(D_MODEL), row(LANES), const((1, LANES))],
        out_shape=[jax.ShapeDtypeStruct((N_TOK, D_MODEL), F32),
                   jax.ShapeDtypeStruct((N_TOK, D_MODEL), F32),
                   jax.ShapeDtypeStruct((N_TOK, LANES), F32),
                   jax.ShapeDtypeStruct((1, LANES), F32)],
        scratch_shapes=[pltpu.VMEM((1, LANES), F32)],
        compiler_params=pltpu.CompilerParams(
            dimension_semantics=("arbitrary",), vmem_limit_bytes=VMEM_LIMIT),
        name="outproj_router",
    )(x, *ys, w, g, wr_hi, wr_lo)


def _swiglu_rows(h, wg_ref, wu_ref, wd_ref):
    acc = None
    for c0 in range(0, D_FF, FF_CHUNK):
        gate = _dot(h, wg_ref[:, c0:c0 + FF_CHUNK])
        up = _dot(h, wu_ref[:, c0:c0 + FF_CHUNK])
        act = (gate * (1.0 / (1.0 + jnp.exp(-gate))) * up).astype(BF16)
        part = _dot(act, wd_ref[c0:c0 + FF_CHUNK, :])
        acc = part if acc is None else acc + part
    return acc


def _dense_ffn_kernel(x_ref, h_ref, wg_ref, wu_ref, wd_ref, o_ref):
    o_ref[...] = x_ref[...] + _swiglu_rows(h_ref[...], wg_ref, wu_ref, wd_ref)


def _dense_ffn(x1, h, wg, wu, wd):
    row = lambda: pl.BlockSpec((ROW_TILE, D_MODEL), lambda i: (i, 0))
    once = pl.Buffered(1)
    return pl.pallas_call(
        _dense_ffn_kernel,
        grid=(N_TOK // ROW_TILE,),
        in_specs=[
            row(), row(),
            pl.BlockSpec((D_MODEL, D_FF), lambda i: (0, 0), pipeline_mode=once),
            pl.BlockSpec((D_MODEL, D_FF), lambda i: (0, 0), pipeline_mode=once),
            pl.BlockSpec((D_FF, D_MODEL), lambda i: (0, 0), pipeline_mode=once),
        ],
        out_specs=row(),
        out_shape=jax.ShapeDtypeStruct((N_TOK, D_MODEL), F32),
        compiler_params=pltpu.CompilerParams(
            dimension_semantics=("parallel",), vmem_limit_bytes=VMEM_LIMIT),
        name="dense_swiglu",
    )(x1, h, wg, wu, wd)


def _row_copy(src_ref, src_row, dst_ref, dst_row, sem):
    return pltpu.make_async_copy(src_ref.at[pl.ds(src_row, 1), :], dst_ref.at[pl.ds(dst_row, 1), :], sem)


def _dispatch_kernel(dest_ref, h_ref, xs_in_ref, xs_ref, sem):
    del xs_in_ref
    base = pl.program_id(0) * MOE_BLOCK

    def issue(r, carry):
        for slot in range(2):
            _row_copy(h_ref, r, xs_ref, dest_ref[2 * (base + r) + slot], sem).start()
        return carry

    lax.fori_loop(0, MOE_BLOCK, issue, 0)

    def drain(r, carry):
        for slot in range(2):
            _row_copy(h_ref, r, xs_ref, dest_ref[2 * (base + r) + slot], sem).wait()
        return carry

    lax.fori_loop(0, MOE_BLOCK, drain, 0)


def _dispatch(dest_flat, h, xs_zero):
    return pl.pallas_call(
        _dispatch_kernel,
        grid_spec=pltpu.PrefetchScalarGridSpec(
            num_scalar_prefetch=1,
            grid=(N_TOK // MOE_BLOCK,),
            in_specs=[pl.BlockSpec((MOE_BLOCK, D_MODEL), lambda i, dest: (i, 0)),
                      pl.BlockSpec(memory_space=pl.ANY)],
            out_specs=pl.BlockSpec(memory_space=pl.ANY),
            scratch_shapes=[pltpu.SemaphoreType.DMA(())],
        ),
        out_shape=jax.ShapeDtypeStruct((MOE_CAP, D_MODEL), F32),
        input_output_aliases={2: 0},
        compiler_params=pltpu.CompilerParams(
            dimension_semantics=("arbitrary",), vmem_limit_bytes=VMEM_LIMIT),
        name="moe_dispatch",
    )(dest_flat, h, xs_zero)


def _expert_kernel(block_e_ref, n_used_ref, xs_ref, wg_ref, wu_ref, wd_ref, y_ref):
    del block_e_ref
    i = pl.program_id(0)

    @pl.when(i < n_used_ref[0])
    def _():
        y_ref[...] = _swiglu_rows(xs_ref[...].astype(BF16), wg_ref, wu_ref, wd_ref)

    @pl.when(i >= n_used_ref[0])
    def _():
        y_ref[...] = jnp.zeros_like(y_ref)


def _experts(block_e, n_used, xs, wg, wu, wd):
    return pl.pallas_call(
        _expert_kernel,
        grid_spec=pltpu.PrefetchScalarGridSpec(
            num_scalar_prefetch=2,
            grid=(N_MOE_BLOCKS,),
            in_specs=[
                pl.BlockSpec((MOE_BLOCK, D_MODEL), lambda i, be, nu: (i, 0)),
                pl.BlockSpec((None, D_MODEL, D_FF), lambda i, be, nu: (be[i], 0, 0)),
                pl.BlockSpec((None, D_MODEL, D_FF), lambda i, be, nu: (be[i], 0, 0)),
                pl.BlockSpec((None, D_FF, D_MODEL), lambda i, be, nu: (be[i], 0, 0)),
            ],
            out_specs=pl.BlockSpec((MOE_BLOCK, D_MODEL), lambda i, be, nu: (i, 0)),
        ),
        out_shape=jax.ShapeDtypeStruct((MOE_CAP, D_MODEL), F32),
        compiler_params=pltpu.CompilerParams(
            dimension_semantics=("arbitrary",), vmem_limit_bytes=VMEM_LIMIT),
        name="moe_experts",
    )(block_e, n_used, xs, wg, wu, wd)


def _combine_kernel(dest_ref, x_ref, route_ref, g_ref, y_ref, o_ref, buf_ref, sem):
    base = pl.program_id(0) * MOE_BLOCK

    def issue(r, carry):
        for slot in range(2):
            _row_copy(y_ref, dest_ref[2 * (base + r) + slot], buf_ref.at[slot], r, sem).start()
        return carry

    lax.fori_loop(0, MOE_BLOCK, issue, 0)

    def drain(r, carry):
        for slot in range(2):
            _row_copy(y_ref, dest_ref[2 * (base + r) + slot], buf_ref.at[slot], r, sem).wait()
        return carry

    lax.fori_loop(0, MOE_BLOCK, drain, 0)

    route = route_ref[...]
    g1 = route[:, _R_G1:_R_G1 + 1]
    g2 = route[:, _R_G2:_R_G2 + 1]
    x2 = x_ref[...] + (buf_ref[0] * g1 + buf_ref[1] * g2)
    o_ref[...] = _rms(x2, g_ref[...])


def _combine_final_norm(dest_flat, x1, route, g_final, ybuf):
    return pl.pallas_call(
        _combine_kernel,
        grid_spec=pltpu.PrefetchScalarGridSpec(
            num_scalar_prefetch=1,
            grid=(N_TOK // MOE_BLOCK,),
            in_specs=[pl.BlockSpec((MOE_BLOCK, D_MODEL), lambda i, dest: (i, 0)),
                      pl.BlockSpec((MOE_BLOCK, LANES), lambda i, dest: (i, 0)),
                      pl.BlockSpec((1, D_MODEL), lambda i, dest: (0, 0)),
                      pl.BlockSpec(memory_space=pl.ANY)],
            out_specs=pl.BlockSpec((MOE_BLOCK, D_MODEL), lambda i, dest: (i, 0)),
            scratch_shapes=[pltpu.VMEM((2, MOE_BLOCK, D_MODEL), F32), pltpu.SemaphoreType.DMA(())],
        ),
        out_shape=jax.ShapeDtypeStruct((N_TOK, D_MODEL), F32),
        compiler_params=pltpu.CompilerParams(
            dimension_semantics=("arbitrary",), vmem_limit_bytes=VMEM_LIMIT),
        name="moe_combine_norm",
    )(dest_flat, x1, route, g_final, ybuf)


def _final_norm_kernel(x_ref, g_ref, o_ref):
    o_ref[...] = _rms(x_ref[...], g_ref[...])


def _final_norm(x, g):
    return pl.pallas_call(
        _final_norm_kernel,
        grid=(N_TOK // ROW_TILE,),
        in_specs=[pl.BlockSpec((ROW_TILE, D_MODEL), lambda i: (i, 0)),
                  pl.BlockSpec((1, D_MODEL), lambda i: (0, 0))],
        out_specs=pl.BlockSpec((ROW_TILE, D_MODEL), lambda i: (i, 0)),
        out_shape=jax.ShapeDtypeStruct((N_TOK, D_MODEL), F32),
        compiler_params=pltpu.CompilerParams(dimension_semantics=("parallel",)),
        name="final_norm",
    )(x, g)


def _expert_row_plan(route, counts):
    experts = route[:, _R_E1:_R_E2 + 1].astype(jnp.int32)
    ranks = route[:, _R_RANK1:_R_RANK2 + 1].astype(jnp.int32)
    counts = counts[0, :N_EXPERTS].astype(jnp.int32)
    padded = ((counts + MOE_BLOCK - 1) // MOE_BLOCK) * MOE_BLOCK
    ends = jnp.cumsum(padded)
    starts = ends - padded
    start_of = jnp.sum(jnp.where(experts[..., None] == jnp.arange(N_EXPERTS), starts, 0), axis=-1)
    dest = (start_of + ranks).reshape(-1)
    block_row = jnp.arange(N_MOE_BLOCKS, dtype=jnp.int32) * MOE_BLOCK
    block_e = jnp.minimum(jnp.sum(block_row[:, None] >= ends[None, :], axis=-1), N_EXPERTS - 1)
    n_used = (ends[-1:] // MOE_BLOCK).astype(jnp.int32)
    return dest.astype(jnp.int32), block_e.astype(jnp.int32), n_used


def kernel(x, g_mix, w_in, sgu_ln_g, sgu_w, sgu_b, conv_w, w_out, g_ffn, dense_w_gate, dense_w_up,
           dense_w_down, router_w, moe_w_gate, moe_w_up, moe_w_down, g_final):
    x = x.reshape(N_TOK, D_MODEL)
    bias_table = _dilated_bias_table()
    out = None
    for l in range(DEPTH):
        za, zb, zc, zd = _norm_inproj(x, g_mix[l].reshape(1, D_MODEL), w_in[l].astype(BF16))
        ya = _moba(za)
        yd = _dilated(zd, bias_table)
        yb, yc = _sgu_conv(zb, zc, sgu_ln_g[l], sgu_w[l], sgu_b[l], conv_w[l])
        ys = (ya, yb, yc, yd)
        g2 = g_ffn[l].reshape(1, D_MODEL)
        j = l // 2
        if l % 2 == 0:
            x1, h = _outproj(x, ys, w_out[l].astype(BF16), g2)
            x = _dense_ffn(x1, h, dense_w_gate[j].astype(BF16), dense_w_up[j].astype(BF16),
                           dense_w_down[j].astype(BF16))
            if l == DEPTH - 1:
                out = _final_norm(x, g_final.reshape(1, D_MODEL))
        else:
            wr = jnp.pad(router_w[j], ((0, 0), (0, LANES - N_EXPERTS)))
            x1, h, route, counts = _outproj(x, ys, w_out[l].astype(BF16), g2, router=_split_bf16(wr))
            dest, block_e, n_used = _expert_row_plan(route, counts)
            xs = _dispatch(dest, h, jnp.zeros((MOE_CAP, D_MODEL), F32))
            ybuf = _experts(block_e, n_used, xs, moe_w_gate[j].astype(BF16), moe_w_up[j].astype(BF16),
                            moe_w_down[j].astype(BF16))
            if l == DEPTH - 1:
                out = _combine_final_norm(dest, x1, route, g_final.reshape(1, D_MODEL), ybuf)
            else:
                raise NotImplementedError("an expert layer that is not the last layer")
    return out.reshape(BATCH, SEQ, D_MODEL)
```

```python
import functools

import numpy as np
import jax
import jax.numpy as jnp
from jax import lax
from jax.experimental import pallas as pl
from jax.experimental.pallas import tpu as pltpu

D_MODEL = 1024
BATCH = 8
SEQ = 2048
DEPTH = 2
N_TOK = BATCH * SEQ
HEAD_DIM = 64
N_HEADS = 4
W_MIX = N_HEADS * HEAD_DIM
CUT_A = 3 * W_MIX
CUT_B = CUT_A + 2 * W_MIX
CUT_C = CUT_B + 3 * W_MIX
IN_COLS = CUT_C + 3 * W_MIX
ATT_BLOCK = 256
N_ATT_BLOCKS = SEQ // ATT_BLOCK
MOBA_TOPK = 3
SGU_CHUNK = 128
DILATED_PATTERNS = ((128, 1), (512, 4), (2048, 16))
D_FF = 2816
FF_CHUNK = D_FF // 2
N_EXPERTS = 8
MOE_BLOCK = 256
N_MOE_BLOCKS = (N_TOK * 2) // MOE_BLOCK + N_EXPERTS
MOE_CAP = N_MOE_BLOCKS * MOE_BLOCK
RMS_EPS = 1e-6
LN_EPS = 1e-5
ATTN_SCALE = HEAD_DIM ** -0.5
MASK_BIAS = -1e30

LANES = 128
ROW_TILE = 512
VMEM_LIMIT = 56 * 1024 * 1024

F32 = jnp.float32
BF16 = jnp.bfloat16


def _dot(a, b):
    return jnp.dot(a, b, preferred_element_type=F32)


def _dot_nt(a, b):
    return lax.dot_general(a, b, (((1,), (1,)), ((), ())), preferred_element_type=F32)


def _rms(x, g):
    return x * lax.rsqrt(jnp.mean(x * x, axis=-1, keepdims=True) + RMS_EPS) * g


def _split_bf16(x):
    hi = x.astype(BF16)
    lo = (x - hi.astype(F32)).astype(BF16)
    return hi, lo


def _norm_inproj_kernel(x_ref, g_ref, w_ref, za_ref, zb_ref, zc_ref, zd_ref):
    h = _rms(x_ref[...], g_ref[...]).astype(BF16)
    za_ref[...] = _dot(h, w_ref[:, 0:CUT_A]).astype(BF16)
    zb_ref[...] = _dot(h, w_ref[:, CUT_A:CUT_B]).astype(BF16)
    zc_ref[...] = _dot(h, w_ref[:, CUT_B:CUT_C]).astype(BF16)
    zd_ref[...] = _dot(h, w_ref[:, CUT_C:IN_COLS]).astype(BF16)


def _norm_inproj(x, g, w):
    widths = (CUT_A, CUT_B - CUT_A, CUT_C - CUT_B, IN_COLS - CUT_C)
    return pl.pallas_call(
        _norm_inproj_kernel,
        grid=(N_TOK // ROW_TILE,),
        in_specs=[
            pl.BlockSpec((ROW_TILE, D_MODEL), lambda i: (i, 0)),
            pl.BlockSpec((1, D_MODEL), lambda i: (0, 0)),
            pl.BlockSpec((D_MODEL, IN_COLS), lambda i: (0, 0)),
        ],
        out_specs=[pl.BlockSpec((ROW_TILE, w_), lambda i: (i, 0)) for w_ in widths],
        out_shape=[jax.ShapeDtypeStruct((N_TOK, w_), BF16) for w_ in widths],
        compiler_params=pltpu.CompilerParams(
            dimension_semantics=("parallel",), vmem_limit_bytes=VMEM_LIMIT),
        name="norm_inproj",
    )(x, g, w)


def _head_lane_mask(h, width):
    lane = lax.broadcasted_iota(jnp.int32, (ATT_BLOCK, width), 1)
    return (lane >= h * HEAD_DIM) & (lane < (h + 1) * HEAD_DIM)


def _store_transposed_v(z_ref, vt_ref):
    for j in range(N_ATT_BLOCKS):
        v = z_ref[j * ATT_BLOCK:(j + 1) * ATT_BLOCK, 2 * W_MIX:3 * W_MIX].astype(F32)
        vt_ref[j] = v.T.astype(BF16)


def _stacked_queries(z_ref, row0):
    qs = z_ref[pl.ds(row0, ATT_BLOCK), 0:W_MIX].astype(F32) * ATTN_SCALE
    return jnp.concatenate(
        [jnp.where(_head_lane_mask(h, W_MIX), qs, 0.0) for h in range(N_HEADS)], axis=0).astype(BF16)


def _head_cols(h):
    return slice(h * ATT_BLOCK, (h + 1) * ATT_BLOCK)


def _values_times_probs(vt_ref, blk, p):
    return jnp.concatenate(
        [_dot(vt_ref[blk, h * HEAD_DIM:(h + 1) * HEAD_DIM, :], p[:, _head_cols(h)]) for h in range(N_HEADS)],
        axis=0)


def _per_head_rows(a):
    return jnp.concatenate(
        [jnp.broadcast_to(a[:, _head_cols(h)], (HEAD_DIM, ATT_BLOCK)) for h in range(N_HEADS)], axis=0)


def _softmax_first(s, vt_ref, blk):
    m = jnp.max(s, axis=0, keepdims=True)
    p = jnp.exp(s - m)
    return m, jnp.sum(p, axis=0, keepdims=True), _values_times_probs(vt_ref, blk, p.astype(BF16))


def _softmax_step(s, bias_row, vt_ref, blk, state):
    m, l, acc = state
    m_new = jnp.maximum(m, jnp.max(s, axis=0, keepdims=True) + bias_row)
    alpha = jnp.exp(m - m_new)
    p = jnp.exp(s - (m_new - bias_row))
    l = alpha * l + jnp.sum(p, axis=0, keepdims=True)
    acc = _per_head_rows(alpha) * acc + _values_times_probs(vt_ref, blk, p.astype(BF16))
    return m_new, l, acc


def _attend(i, own_scores, scores, bias_row, vt_ref, s_refs, o_ref):
    s_a, s_b = s_refs
    last = jnp.maximum(i - 1, 0)
    s_a[...] = own_scores
    s_b[...] = scores(0)
    state = _softmax_first(s_a[...], vt_ref, i)
    s_a[...] = scores(jnp.minimum(1, last))
    state = _softmax_step(s_b[...], bias_row(0), vt_ref, 0, state)

    def two_blocks(t, state):
        j0 = 2 * t + 1
        j1 = j0 + 1
        s_b[...] = scores(jnp.minimum(j1, last))
        state = _softmax_step(s_a[...], bias_row(j0), vt_ref, j0, state)
        s_a[...] = scores(jnp.minimum(j1 + 1, last))
        return _softmax_step(s_b[...], bias_row(j1), vt_ref, j1, state)

    _, l, acc = lax.fori_loop(0, i // 2, two_blocks, state)
    o_ref[...] = (acc / _per_head_rows(l)).T.astype(o_ref.dtype)


def _moba_kernel(z_ref, o_ref, kmt_hi_ref, kmt_lo_ref, vt_ref, bias_ref, sa_ref, sb_ref):
    i = pl.program_id(1)

    @pl.when(i == 0)
    def _prepare_batch():
        means = []
        for j in range(N_ATT_BLOCKS):
            kj = z_ref[j * ATT_BLOCK:(j + 1) * ATT_BLOCK, W_MIX:2 * W_MIX].astype(F32)
            means.append(jnp.sum(kj, axis=0, keepdims=True) * (1.0 / ATT_BLOCK))
        mt = jnp.concatenate(means * N_HEADS, axis=0)
        r = lax.broadcasted_iota(jnp.int32, mt.shape, 0)
        c = lax.broadcasted_iota(jnp.int32, mt.shape, 1)
        hi, lo = _split_bf16(jnp.where((c >> 6) == (r >> 3), mt, 0.0))
        kmt_hi_ref[...] = hi
        kmt_lo_ref[...] = lo
        _store_transposed_v(z_ref, vt_ref)

    row0 = pl.multiple_of(i * ATT_BLOCK, ATT_BLOCK)
    q = z_ref[pl.ds(row0, ATT_BLOCK), 0:W_MIX]
    q_stack = _stacked_queries(z_ref, row0)

    def scores(blk):
        r0 = pl.multiple_of(blk * ATT_BLOCK, ATT_BLOCK)
        return _dot_nt(z_ref[pl.ds(r0, ATT_BLOCK), W_MIX:2 * W_MIX], q_stack)

    gates = _dot_nt(kmt_hi_ref[...], q) + _dot_nt(kmt_lo_ref[...], q)
    blk = lax.broadcasted_iota(jnp.int32, (N_ATT_BLOCKS, ATT_BLOCK), 0)
    past = blk < i
    biases = []
    for h in range(N_HEADS):
        g = jnp.where(past, gates[h * N_ATT_BLOCKS:(h + 1) * N_ATT_BLOCKS], -jnp.inf)
        rank = jnp.zeros(g.shape, F32)
        for d in range(1, N_ATT_BLOCKS):
            lower = pltpu.roll(g, d, axis=0)
            rank = rank + jnp.where((blk >= d) & (lower >= g), 1.0, 0.0)
            upper = pltpu.roll(g, N_ATT_BLOCKS - d, axis=0)
            rank = rank + jnp.where((blk + d < N_ATT_BLOCKS) & (upper > g), 1.0, 0.0)
        biases.append(jnp.where(past & (rank < MOBA_TOPK), 0.0, MASK_BIAS))
    bias = jnp.concatenate(biases, axis=1)
    for j in range(N_ATT_BLOCKS):
        bias_ref[j] = bias[j:j + 1]

    ki = lax.broadcasted_iota(jnp.int32, (ATT_BLOCK, N_HEADS * ATT_BLOCK), 0)
    qi = lax.broadcasted_iota(jnp.int32, (ATT_BLOCK, N_HEADS * ATT_BLOCK), 1) & (ATT_BLOCK - 1)
    own = jnp.where(ki <= qi, scores(i), -jnp.inf)
    _attend(i, own, scores, lambda j: bias_ref[j], vt_ref, (sa_ref, sb_ref), o_ref)


def _moba(za):
    za3 = za.reshape(BATCH, SEQ, CUT_A)
    out = pl.pallas_call(
        _moba_kernel,
        grid=(BATCH, N_ATT_BLOCKS),
        in_specs=[pl.BlockSpec((None, SEQ, CUT_A), lambda b, i: (b, 0, 0))],
        out_specs=pl.BlockSpec((None, ATT_BLOCK, W_MIX), lambda b, i: (b, i, 0)),
        out_shape=jax.ShapeDtypeStruct((BATCH, SEQ, W_MIX), BF16),
        scratch_shapes=[
            pltpu.VMEM((N_HEADS * N_ATT_BLOCKS, W_MIX), BF16),
            pltpu.VMEM((N_HEADS * N_ATT_BLOCKS, W_MIX), BF16),
            pltpu.VMEM((N_ATT_BLOCKS, W_MIX, ATT_BLOCK), BF16),
            pltpu.VMEM((N_ATT_BLOCKS, 1, N_HEADS * ATT_BLOCK), F32),
            pltpu.VMEM((ATT_BLOCK, N_HEADS * ATT_BLOCK), F32),
            pltpu.VMEM((ATT_BLOCK, N_HEADS * ATT_BLOCK), F32),
        ],
        compiler_params=pltpu.CompilerParams(
            dimension_semantics=("parallel", "arbitrary"), vmem_limit_bytes=VMEM_LIMIT),
        name="moba_attention",
    )(za3)
    return out.reshape(N_TOK, W_MIX)


def _dilated_bias_table():
    d = np.arange(ATT_BLOCK)[None, :] - np.arange(ATT_BLOCK)[:, None]
    tiles = []
    for k in range(N_ATT_BLOCKS):
        dist = d + k * ATT_BLOCK
        count = np.zeros(dist.shape, np.float64)
        for window, dil in DILATED_PATTERNS:
            count += (dist >= 0) & (dist <= window) & (dist % dil == 0)
        with np.errstate(divide="ignore"):
            tiles.append(np.log(count))
    return jnp.asarray(np.stack(tiles), F32)


def _dilated_kernel(z_ref, bias_ref, o_ref, vt_ref, sa_ref, sb_ref):
    i = pl.program_id(1)

    @pl.when(i == 0)
    def _prepare_batch():
        _store_transposed_v(z_ref, vt_ref)

    row0 = pl.multiple_of(i * ATT_BLOCK, ATT_BLOCK)
    q_stack = _stacked_queries(z_ref, row0)

    def scores(blk):
        r0 = pl.multiple_of(blk * ATT_BLOCK, ATT_BLOCK)
        s = _dot_nt(z_ref[pl.ds(r0, ATT_BLOCK), W_MIX:2 * W_MIX], q_stack)
        b = bias_ref[i - blk]
        return jnp.concatenate([s[:, _head_cols(h)] + b for h in range(N_HEADS)], axis=1)

    def bias_row(j):
        return jnp.full((1, N_HEADS * ATT_BLOCK), jnp.where(j < i, 0.0, MASK_BIAS), F32)

    _attend(i, scores(i), scores, bias_row, vt_ref, (sa_ref, sb_ref), o_ref)


def _dilated(zd, bias_table):
    zd3 = zd.reshape(BATCH, SEQ, 3 * W_MIX)
    out = pl.pallas_call(
        _dilated_kernel,
        grid=(BATCH, N_ATT_BLOCKS),
        in_specs=[
            pl.BlockSpec((None, SEQ, 3 * W_MIX), lambda b, i: (b, 0, 0)),
            pl.BlockSpec((N_ATT_BLOCKS, ATT_BLOCK, ATT_BLOCK), lambda b, i: (0, 0, 0)),
        ],
        out_specs=pl.BlockSpec((None, ATT_BLOCK, W_MIX), lambda b, i: (b, i, 0)),
        out_shape=jax.ShapeDtypeStruct((BATCH, SEQ, W_MIX), BF16),
        scratch_shapes=[pltpu.VMEM((N_ATT_BLOCKS, W_MIX, ATT_BLOCK), BF16),
                        pltpu.VMEM((ATT_BLOCK, N_HEADS * ATT_BLOCK), F32),
                        pltpu.VMEM((ATT_BLOCK, N_HEADS * ATT_BLOCK), F32)],
        compiler_params=pltpu.CompilerParams(
            dimension_semantics=("parallel", "arbitrary"), vmem_limit_bytes=VMEM_LIMIT),
        name="dilated_attention",
    )(zd3, bias_table)
    return out.reshape(N_TOK, W_MIX)


def _gelu_tanh(x):
    return 0.5 * x * (1.0 + jnp.tanh(np.sqrt(2.0 / np.pi).astype(np.float32) * (x + 0.044715 * (x * x * x))))


def _sgu_conv_kernel(zb_ref, zc_ref, lng_ref, ws_ref, bs_ref, cw_ref, yb_ref, yc_ref):
    r = lax.broadcasted_iota(jnp.int32, (N_HEADS * SGU_CHUNK, SGU_CHUNK), 0)
    c = lax.broadcasted_iota(jnp.int32, (N_HEADS * SGU_CHUNK, SGU_CHUNK), 1)
    w_stack = jnp.where((r & (SGU_CHUNK - 1)) >= c, ws_ref[...], 0.0).astype(BF16)
    group = lax.broadcasted_iota(jnp.int32, (SGU_CHUNK, W_MIX), 1) >> 6
    ln_g = lng_ref[...]
    b_tile = bs_ref[...]

    def chunk(n, carry):
        r0 = pl.multiple_of(n * SGU_CHUNK, SGU_CHUNK)
        gz = _gelu_tanh(zb_ref[pl.ds(r0, SGU_CHUNK), :].astype(F32))
        u = gz[:, 0:W_MIX]
        v = gz[:, W_MIX:2 * W_MIX]
        vc = v - jnp.mean(v, axis=-1, keepdims=True)
        vn = vc * lax.rsqrt(jnp.mean(vc * vc, axis=-1, keepdims=True) + LN_EPS) * ln_g
        mixed = _dot(w_stack, vn.astype(BF16))
        sv = mixed[0:SGU_CHUNK]
        for gi in range(1, N_HEADS):
            sv = jnp.where(group == gi, mixed[gi * SGU_CHUNK:(gi + 1) * SGU_CHUNK], sv)
        yb_ref[pl.ds(r0, SGU_CHUNK), :] = (u * (sv + b_tile)).astype(yb_ref.dtype)
        return carry

    lax.fori_loop(0, SEQ // SGU_CHUNK, chunk, 0)

    zc = zc_ref[...].astype(F32)
    z = zc[:, W_MIX:2 * W_MIX] * zc[:, 2 * W_MIX:3 * W_MIX]
    t = lax.broadcasted_iota(jnp.int32, (SEQ, W_MIX), 0)
    z1 = jnp.where(t >= 1, pltpu.roll(z, 1, axis=0), 0.0)
    z2 = jnp.where(t >= 2, pltpu.roll(z, 2, axis=0), 0.0)
    y = z2 * cw_ref[0:1, :]
    y = y + z1 * cw_ref[1:2, :]
    y = y + z * cw_ref[2:3, :]
    yc_ref[...] = (zc[:, 0:W_MIX] * y).astype(yc_ref.dtype)


def _sgu_conv(zb, zc, ln_g, w_s, b_s, conv_w):
    w_stack = w_s.reshape(N_HEADS * SGU_CHUNK, SGU_CHUNK)
    b_tile = jnp.repeat(b_s.T, HEAD_DIM, axis=1)
    yb, yc = pl.pallas_call(
        _sgu_conv_kernel,
        grid=(BATCH,),
        in_specs=[
            pl.BlockSpec((None, SEQ, 2 * W_MIX), lambda b: (b, 0, 0)),
            pl.BlockSpec((None, SEQ, 3 * W_MIX), lambda b: (b, 0, 0)),
            pl.BlockSpec((1, W_MIX), lambda b: (0, 0)),
            pl.BlockSpec((N_HEADS * SGU_CHUNK, SGU_CHUNK), lambda b: (0, 0)),
            pl.BlockSpec((SGU_CHUNK, W_MIX), lambda b: (0, 0)),
            pl.BlockSpec((3, W_MIX), lambda b: (0, 0)),
        ],
        out_specs=[pl.BlockSpec((None, SEQ, W_MIX), lambda b: (b, 0, 0))] * 2,
        out_shape=[jax.ShapeDtypeStruct((BATCH, SEQ, W_MIX), BF16)] * 2,
        compiler_params=pltpu.CompilerParams(
            dimension_semantics=("parallel",), vmem_limit_bytes=VMEM_LIMIT),
        name="sgu_conv",
    )(zb.reshape(BATCH, SEQ, 2 * W_MIX), zc.reshape(BATCH, SEQ, 3 * W_MIX),
      ln_g.reshape(1, W_MIX), w_stack, b_tile, conv_w)
    return yb.reshape(N_TOK, W_MIX), yc.reshape(N_TOK, W_MIX)


def _outproj_residual(x_ref, ya_ref, yb_ref, yc_ref, yd_ref, w_ref):
    acc = _dot(ya_ref[...], w_ref[0:W_MIX, :])
    acc = acc + _dot(yb_ref[...], w_ref[W_MIX:2 * W_MIX, :])
    acc = acc + _dot(yc_ref[...], w_ref[2 * W_MIX:3 * W_MIX, :])
    acc = acc + _dot(yd_ref[...], w_ref[3 * W_MIX:4 * W_MIX, :])
    return x_ref[...] + acc


def _outproj_dense_kernel(x_ref, ya_ref, yb_ref, yc_ref, yd_ref, w_ref, g_ref, x1_ref, h_ref):
    x1 = _outproj_residual(x_ref, ya_ref, yb_ref, yc_ref, yd_ref, w_ref)
    x1_ref[...] = x1
    h_ref[...] = _rms(x1, g_ref[...]).astype(BF16)


_R_E1, _R_E2, _R_G1, _R_G2, _R_RANK1, _R_RANK2 = range(6)


def _outproj_router_kernel(x_ref, ya_ref, yb_ref, yc_ref, yd_ref, w_ref, g_ref, wr_hi_ref, wr_lo_ref,
                           x1_ref, h_ref, route_ref, counts_ref, running_ref):
    step = pl.program_id(0)

    @pl.when(step == 0)
    def _():
        running_ref[...] = jnp.zeros_like(running_ref)

    x1 = _outproj_residual(x_ref, ya_ref, yb_ref, yc_ref, yd_ref, w_ref)
    x1_ref[...] = x1
    h = _rms(x1, g_ref[...])
    h_ref[...] = h

    h_hi, h_lo = _split_bf16(h)
    logits = _dot(h_hi, wr_hi_ref[...]) + (_dot(h_lo, wr_hi_ref[...]) + _dot(h_hi, wr_lo_ref[...]))
    lane = lax.broadcasted_iota(jnp.int32, (ROW_TILE, LANES), 1)
    lane_f = lane.astype(F32)
    logits = jnp.where(lane < N_EXPERTS, logits, -jnp.inf)
    m1 = jnp.max(logits, axis=-1, keepdims=True)
    e1 = jnp.min(jnp.where(logits == m1, lane_f, float(LANES)), axis=-1, keepdims=True)
    rest = jnp.where(lane_f == e1, -jnp.inf, logits)
    m2 = jnp.max(rest, axis=-1, keepdims=True)
    e2 = jnp.min(jnp.where(rest == m2, lane_f, float(LANES)), axis=-1, keepdims=True)
    t = jnp.exp(m2 - m1)
    g1 = 1.0 / (1.0 + t)
    g2 = t / (1.0 + t)

    chosen = jnp.where((lane_f == e1) | (lane_f == e2), 1.0, 0.0)
    ri = lax.broadcasted_iota(jnp.int32, (ROW_TILE, ROW_TILE), 0)
    ci = lax.broadcasted_iota(jnp.int32, (ROW_TILE, ROW_TILE), 1)
    earlier = jnp.where(ci < ri, 1.0, 0.0).astype(BF16)
    before = _dot(earlier, chosen.astype(BF16)) + running_ref[...]
    rank1 = jnp.sum(jnp.where(lane_f == e1, before, 0.0), axis=-1, keepdims=True)
    rank2 = jnp.sum(jnp.where(lane_f == e2, before, 0.0), axis=-1, keepdims=True)
    running_ref[...] = running_ref[...] + jnp.sum(chosen, axis=0, keepdims=True)
    counts_ref[...] = running_ref[...]

    rec = jnp.zeros((ROW_TILE, LANES), F32)
    for pos, val in ((_R_E1, e1), (_R_E2, e2), (_R_G1, g1), (_R_G2, g2),
                     (_R_RANK1, rank1), (_R_RANK2, rank2)):
        rec = jnp.where(lane == pos, val, rec)
    route_ref[...] = rec


def _outproj(x, ys, w, g, router=None):
    row = lambda width: pl.BlockSpec((ROW_TILE, width), lambda i: (i, 0))
    const = lambda shape: pl.BlockSpec(shape, lambda i: (0, 0))
    in_specs = [row(D_MODEL)] + [row(W_MIX)] * 4 + [const((D_MODEL, D_MODEL)), const((1, D_MODEL))]
    if router is None:
        return pl.pallas_call(
            _outproj_dense_kernel,
            grid=(N_TOK // ROW_TILE,),
            in_specs=in_specs,
            out_specs=[row(D_MODEL), row(D_MODEL)],
            out_shape=[jax.ShapeDtypeStruct((N_TOK, D_MODEL), F32),
                       jax.ShapeDtypeStruct((N_TOK, D_MODEL), BF16)],
            compiler_params=pltpu.CompilerParams(
                dimension_semantics=("parallel",), vmem_limit_bytes=VMEM_LIMIT),
            name="outproj_norm",
        )(x, *ys, w, g)
    wr_hi, wr_lo = router
    return pl.pallas_call(
        _outproj_router_kernel,
        grid=(N_TOK // ROW_TILE,),
        in_specs=in_specs + [const((D_MODEL, LANES)), const((D_MODEL, LANES))],
        out_specs=[row(D_MODEL), row(D_MODEL), row(LANES), const((1, LANES))],
        out_shape=[jax.ShapeDtypeStruct((N_TOK, D_MODEL), F32),
                   jax.ShapeDtypeStruct((N_TOK, D_MODEL), F32),
                   jax.ShapeDtypeStruct((N_TOK, LANES), F32),
                   jax.ShapeDtypeStruct((1, LANES), F32)],
        scratch_shapes=[pltpu.VMEM((1, LANES), F32)],
        compiler_params=pltpu.CompilerParams(
            dimension_semantics=("arbitrary",), vmem_limit_bytes=VMEM_LIMIT),
        name="outproj_router",
    )(x, *ys, w, g, wr_hi, wr_lo)


def _swiglu_rows(h, wg_ref, wu_ref, wd_ref):
    acc = None
    for c0 in range(0, D_FF, FF_CHUNK):
        gate = _dot(h, wg_ref[:, c0:c0 + FF_CHUNK])
        up = _dot(h, wu_ref[:, c0:c0 + FF_CHUNK])
        act = (gate * (1.0 / (1.0 + jnp.exp(-gate))) * up).astype(BF16)
        part = _dot(act, wd_ref[c0:c0 + FF_CHUNK, :])
        acc = part if acc is None else acc + part
    return acc


def _dense_ffn_kernel(x_ref, h_ref, wg_ref, wu_ref, wd_ref, o_ref):
    o_ref[...] = x_ref[...] + _swiglu_rows(h_ref[...], wg_ref, wu_ref, wd_ref)


def _dense_ffn(x1, h, wg, wu, wd):
    row = lambda: pl.BlockSpec((ROW_TILE, D_MODEL), lambda i: (i, 0))
    once = pl.Buffered(1)
    return pl.pallas_call(
        _dense_ffn_kernel,
        grid=(N_TOK // ROW_TILE,),
        in_specs=[
            row(), row(),
            pl.BlockSpec((D_MODEL, D_FF), lambda i: (0, 0), pipeline_mode=once),
            pl.BlockSpec((D_MODEL, D_FF), lambda i: (0, 0), pipeline_mode=once),
            pl.BlockSpec((D_FF, D_MODEL), lambda i: (0, 0), pipeline_mode=once),
        ],
        out_specs=row(),
        out_shape=jax.ShapeDtypeStruct((N_TOK, D_MODEL), F32),
        compiler_params=pltpu.CompilerParams(
            dimension_semantics=("parallel",), vmem_limit_bytes=VMEM_LIMIT),
        name="dense_swiglu",
    )(x1, h, wg, wu, wd)


def _row_copy(src_ref, src_row, dst_ref, dst_row, sem):
    return pltpu.make_async_copy(src_ref.at[pl.ds(src_row, 1), :], dst_ref.at[pl.ds(dst_row, 1), :], sem)


def _dispatch_kernel(dest_ref, h_ref, xs_in_ref, xs_ref, sem):
    del xs_in_ref
    base = pl.program_id(0) * MOE_BLOCK

    def issue(r, carry):
        for slot in range(2):
            _row_copy(h_ref, r, xs_ref, dest_ref[2 * (base + r) + slot], sem).start()
        return carry

    lax.fori_loop(0, MOE_BLOCK, issue, 0)

    def drain(r, carry):
        for slot in range(2):
            _row_copy(h_ref, r, xs_ref, dest_ref[2 * (base + r) + slot], sem).wait()
        return carry

    lax.fori_loop(0, MOE_BLOCK, drain, 0)


def _dispatch(dest_flat, h, xs_zero):
    return pl.pallas_call(
        _dispatch_kernel,
        grid_spec=pltpu.PrefetchScalarGridSpec(
            num_scalar_prefetch=1,
            grid=(N_TOK // MOE_BLOCK,),
            in_specs=[pl.BlockSpec((MOE_BLOCK, D_MODEL), lambda i, dest: (i, 0)),
                      pl.BlockSpec(memory_space=pl.ANY)],
            out_specs=pl.BlockSpec(memory_space=pl.ANY),
            scratch_shapes=[pltpu.SemaphoreType.DMA(())],
        ),
        out_shape=jax.ShapeDtypeStruct((MOE_CAP, D_MODEL), F32),
        input_output_aliases={2: 0},
        compiler_params=pltpu.CompilerParams(
            dimension_semantics=("arbitrary",), vmem_limit_bytes=VMEM_LIMIT),
        name="moe_dispatch",
    )(dest_flat, h, xs_zero)


def _expert_kernel(block_e_ref, n_used_ref, xs_ref, wg_ref, wu_ref, wd_ref, y_ref):
    del block_e_ref
    i = pl.program_id(0)

    @pl.when(i < n_used_ref[0])
    def _():
        y_ref[...] = _swiglu_rows(xs_ref[...].astype(BF16), wg_ref, wu_ref, wd_ref)

    @pl.when(i >= n_used_ref[0])
    def _():
        y_ref[...] = jnp.zeros_like(y_ref)


def _experts(block_e, n_used, xs, wg, wu, wd):
    return pl.pallas_call(
        _expert_kernel,
        grid_spec=pltpu.PrefetchScalarGridSpec(
            num_scalar_prefetch=2,
            grid=(N_MOE_BLOCKS,),
            in_specs=[
                pl.BlockSpec((MOE_BLOCK, D_MODEL), lambda i, be, nu: (i, 0)),
                pl.BlockSpec((None, D_MODEL, D_FF), lambda i, be, nu: (be[i], 0, 0)),
                pl.BlockSpec((None, D_MODEL, D_FF), lambda i, be, nu: (be[i], 0, 0)),
                pl.BlockSpec((None, D_FF, D_MODEL), lambda i, be, nu: (be[i], 0, 0)),
            ],
            out_specs=pl.BlockSpec((MOE_BLOCK, D_MODEL), lambda i, be, nu: (i, 0)),
        ),
        out_shape=jax.ShapeDtypeStruct((MOE_CAP, D_MODEL), F32),
        compiler_params=pltpu.CompilerParams(
            dimension_semantics=("arbitrary",), vmem_limit_bytes=VMEM_LIMIT),
        name="moe_experts",
    )(block_e, n_used, xs, wg, wu, wd)


def _combine_kernel(dest_ref, x_ref, route_ref, g_ref, y_ref, o_ref, buf_ref, sem):
    base = pl.program_id(0) * MOE_BLOCK

    def issue(r, carry):
        for slot in range(2):
            _row_copy(y_ref, dest_ref[2 * (base + r) + slot], buf_ref.at[slot], r, sem).start()
        return carry

    lax.fori_loop(0, MOE_BLOCK, issue, 0)

    def drain(r, carry):
        for slot in range(2):
            _row_copy(y_ref, dest_ref[2 * (base + r) + slot], buf_ref.at[slot], r, sem).wait()
        return carry

    lax.fori_loop(0, MOE_BLOCK, drain, 0)

    route = route_ref[...]
    g1 = route[:, _R_G1:_R_G1 + 1]
    g2 = route[:, _R_G2:_R_G2 + 1]
    x2 = x_ref[...] + (buf_ref[0] * g1 + buf_ref[1] * g2)
    o_ref[...] = _rms(x2, g_ref[...])


def _combine_final_norm(dest_flat, x1, route, g_final, ybuf):
    return pl.pallas_call(
        _combine_kernel,
        grid_spec=pltpu.PrefetchScalarGridSpec(
            num_scalar_prefetch=1,
            grid=(N_TOK // MOE_BLOCK,),
            in_specs=[pl.BlockSpec((MOE_BLOCK, D_MODEL), lambda i, dest: (i, 0)),
                      pl.BlockSpec((MOE_BLOCK, LANES), lambda i, dest: (i, 0)),
                      pl.BlockSpec((1, D_MODEL), lambda i, dest: (0, 0)),
                      pl.BlockSpec(memory_space=pl.ANY)],
            out_specs=pl.BlockSpec((MOE_BLOCK, D_MODEL), lambda i, dest: (i, 0)),
            scratch_shapes=[pltpu.VMEM((2, MOE_BLOCK, D_MODEL), F32), pltpu.SemaphoreType.DMA(())],
        ),
        out_shape=jax.ShapeDtypeStruct((N_TOK, D_MODEL), F32),
        compiler_params=pltpu.CompilerParams(
            dimension_semantics=("arbitrary",), vmem_limit_bytes=VMEM_LIMIT),
        name="moe_combine_norm",
    )(dest_flat, x1, route, g_final, ybuf)


def _final_norm_kernel(x_ref, g_ref, o_ref):
    o_ref[...] = _rms(x_ref[...], g_ref[...])


def _final_norm(x, g):
    return pl.pallas_call(
        _final_norm_kernel,
        grid=(N_TOK // ROW_TILE,),
        in_specs=[pl.BlockSpec((ROW_TILE, D_MODEL), lambda i: (i, 0)),
                  pl.BlockSpec((1, D_MODEL), lambda i: (0, 0))],
        out_specs=pl.BlockSpec((ROW_TILE, D_MODEL), lambda i: (i, 0)),
        out_shape=jax.ShapeDtypeStruct((N_TOK, D_MODEL), F32),
        compiler_params=pltpu.CompilerParams(dimension_semantics=("parallel",)),
        name="final_norm",
    )(x, g)


def _expert_row_plan(route, counts):
    experts = route[:, _R_E1:_R_E2 + 1].astype(jnp.int32)
    ranks = route[:, _R_RANK1:_R_RANK2 + 1].astype(jnp.int32)
    counts = counts[0, :N_EXPERTS].astype(jnp.int32)
    padded = ((counts + MOE_BLOCK - 1) // MOE_BLOCK) * MOE_BLOCK
    ends = jnp.cumsum(padded)
    starts = ends - padded
    start_of = jnp.sum(jnp.where(experts[..., None] == jnp.arange(N_EXPERTS), starts, 0), axis=-1)
    dest = (start_of + ranks).reshape(-1)
    block_row = jnp.arange(N_MOE_BLOCKS, dtype=jnp.int32) * MOE_BLOCK
    block_e = jnp.minimum(jnp.sum(block_row[:, None] >= ends[None, :], axis=-1), N_EXPERTS - 1)
    n_used = (ends[-1:] // MOE_BLOCK).astype(jnp.int32)
    return dest.astype(jnp.int32), block_e.astype(jnp.int32), n_used


def kernel(x, g_mix, w_in, sgu_ln_g, sgu_w, sgu_b, conv_w, w_out, g_ffn, dense_w_gate, dense_w_up,
           dense_w_down, router_w, moe_w_gate, moe_w_up, moe_w_down, g_final):
    x = x.reshape(N_TOK, D_MODEL)
    bias_table = _dilated_bias_table()
    out = None
    for l in range(DEPTH):
        za, zb, zc, zd = _norm_inproj(x, g_mix[l].reshape(1, D_MODEL), w_in[l].astype(BF16))
        ya = _moba(za)
        yd = _dilated(zd, bias_table)
        yb, yc = _sgu_conv(zb, zc, sgu_ln_g[l], sgu_w[l], sgu_b[l], conv_w[l])
        ys = (ya, yb, yc, yd)
        g2 = g_ffn[l].reshape(1, D_MODEL)
        j = l // 2
        if l % 2 == 0:
            x1, h = _outproj(x, ys, w_out[l].astype(BF16), g2)
            x = _dense_ffn(x1, h, dense_w_gate[j].astype(BF16), dense_w_up[j].astype(BF16),
                           dense_w_down[j].astype(BF16))
            if l == DEPTH - 1:
                out = _final_norm(x, g_final.reshape(1, D_MODEL))
        else:
            wr = jnp.pad(router_w[j], ((0, 0), (0, LANES - N_EXPERTS)))
            x1, h, route, counts = _outproj(x, ys, w_out[l].astype(BF16), g2, router=_split_bf16(wr))
            dest, block_e, n_used = _expert_row_plan(route, counts)
            xs = _dispatch(dest, h, jnp.zeros((MOE_CAP, D_MODEL), F32))
            ybuf = _experts(block_e, n_used, xs, moe_w_gate[j].astype(BF16), moe_w_up[j].astype(BF16),
                            moe_w_down[j].astype(BF16))
            if l == DEPTH - 1:
                out = _combine_final_norm(dest, x1, route, g_final.reshape(1, D_MODEL), ybuf)
            else:
                raise NotImplementedError("an expert layer that is not the last layer")
    return out.reshape(BATCH, SEQ, D_MODEL)
```

```python
import functools

import numpy as np
import jax
import jax.numpy as jnp
from jax import lax
from jax.experimental import pallas as pl
from jax.experimental.pallas import tpu as pltpu

D_MODEL = 1024
BATCH = 8
SEQ = 2048
DEPTH = 2
N_TOK = BATCH * SEQ
HEAD_DIM = 64
N_HEADS = 4
W_MIX = N_HEADS * HEAD_DIM
CUT_A = 3 * W_MIX
CUT_B = CUT_A + 2 * W_MIX
CUT_C = CUT_B + 3 * W_MIX
IN_COLS = CUT_C + 3 * W_MIX
ATT_BLOCK = 256
HEAD_ROWS = HEAD_DIM + 16
N_ATT_BLOCKS = SEQ // ATT_BLOCK
MOBA_TOPK = 3
SGU_CHUNK = 128
DILATED_PATTERNS = ((128, 1), (512, 4), (2048, 16))
D_FF = 2816
FF_CHUNK = D_FF // 2
EXPERT_FF_CHUNK = FF_CHUNK
N_EXPERTS = 8
MOE_BLOCK = 256
N_MOE_BLOCKS = (N_TOK * 2) // MOE_BLOCK + N_EXPERTS
MOE_CAP = N_MOE_BLOCKS * MOE_BLOCK
Y_PLANE = N_TOK + 2 * MOE_BLOCK
RMS_EPS = 1e-6
LN_EPS = 1e-5
ATTN_SCALE = HEAD_DIM ** -0.5
MASK_BIAS = -1e30

LANES = 128
ROW_TILE = 512
VMEM_LIMIT = 56 * 1024 * 1024

F32 = jnp.float32
BF16 = jnp.bfloat16


def _dot(a, b):
    return jnp.dot(a, b, preferred_element_type=F32)


def _dot_nt(a, b):
    return lax.dot_general(a, b, (((1,), (1,)), ((), ())), preferred_element_type=F32)


def _rms(x, g):
    return x * lax.rsqrt(jnp.mean(x * x, axis=-1, keepdims=True) + RMS_EPS) * g


def _split_bf16(x):
    hi = x.astype(BF16)
    lo = (x - hi.astype(F32)).astype(BF16)
    return hi, lo


def _norm_inproj_kernel(x_ref, g_ref, w_ref, za_ref, zb_ref, zc_ref, zd_ref):
    h = _rms(x_ref[...], g_ref[...]).astype(BF16)
    za_ref[...] = _dot(h, w_ref[:, 0:CUT_A]).astype(BF16)
    zb_ref[...] = _dot(h, w_ref[:, CUT_A:CUT_B]).astype(BF16)
    zc_ref[...] = _dot(h, w_ref[:, CUT_B:CUT_C]).astype(BF16)
    zd_ref[...] = _dot(h, w_ref[:, CUT_C:IN_COLS]).astype(BF16)


def _norm_inproj(x, g, w):
    widths = (CUT_A, CUT_B - CUT_A, CUT_C - CUT_B, IN_COLS - CUT_C)
    return pl.pallas_call(
        _norm_inproj_kernel,
        grid=(N_TOK // ROW_TILE,),
        in_specs=[
            pl.BlockSpec((ROW_TILE, D_MODEL), lambda i: (i, 0)),
            pl.BlockSpec((1, D_MODEL), lambda i: (0, 0)),
            pl.BlockSpec((D_MODEL, IN_COLS), lambda i: (0, 0)),
        ],
        out_specs=[pl.BlockSpec((ROW_TILE, w_), lambda i: (i, 0)) for w_ in widths],
        out_shape=[jax.ShapeDtypeStruct((N_TOK, w_), BF16) for w_ in widths],
        compiler_params=pltpu.CompilerParams(
            dimension_semantics=("parallel",), vmem_limit_bytes=VMEM_LIMIT),
        name="norm_inproj",
    )(x, g, w)


def _head_lane_mask(h, width):
    lane = lax.broadcasted_iota(jnp.int32, (ATT_BLOCK, width), 1)
    return (lane >= h * HEAD_DIM) & (lane < (h + 1) * HEAD_DIM)


def _store_transposed_v(z_ref, vt_ref):
    row = lax.broadcasted_iota(jnp.int32, (HEAD_ROWS - HEAD_DIM, ATT_BLOCK), 0)
    ones_row = jnp.where(row == 0, 1.0, 0.0).astype(BF16)
    for j in range(N_ATT_BLOCKS):
        vt = z_ref[j * ATT_BLOCK:(j + 1) * ATT_BLOCK, 2 * W_MIX:3 * W_MIX].astype(F32).T
        for h in range(N_HEADS):
            vt_ref[j, h * HEAD_ROWS:h * HEAD_ROWS + HEAD_DIM, :] = (
                vt[h * HEAD_DIM:(h + 1) * HEAD_DIM].astype(BF16))
            vt_ref[j, h * HEAD_ROWS + HEAD_DIM:(h + 1) * HEAD_ROWS, :] = ones_row


def _stacked_queries(z_ref, row0):
    qs = z_ref[pl.ds(row0, ATT_BLOCK), 0:W_MIX].astype(F32) * ATTN_SCALE
    return jnp.concatenate(
        [jnp.where(_head_lane_mask(h, W_MIX), qs, 0.0) for h in range(N_HEADS)], axis=0).astype(BF16)


def _head_cols(h):
    return slice(h * ATT_BLOCK, (h + 1) * ATT_BLOCK)


def _values_times_probs(vt_ref, blk, p):
    return jnp.concatenate(
        [_dot(vt_ref[blk, h * HEAD_ROWS:(h + 1) * HEAD_ROWS, :], p[:, _head_cols(h)]) for h in range(N_HEADS)],
        axis=0)


def _per_head_rows(a):
    return jnp.concatenate(
        [jnp.broadcast_to(a[:, _head_cols(h)], (HEAD_ROWS, ATT_BLOCK)) for h in range(N_HEADS)], axis=0)


def _softmax_first(s, vt_ref, blk):
    m = jnp.max(s, axis=0, keepdims=True)
    return m, _values_times_probs(vt_ref, blk, jnp.exp(s - m).astype(BF16))


def _softmax_step(s, bias_row, vt_ref, blk, state):
    m, acc = state
    m_new = jnp.maximum(m, jnp.max(s, axis=0, keepdims=True) + bias_row)
    alpha = jnp.exp(m - m_new)
    p = jnp.exp(s - (m_new - bias_row))
    acc = _per_head_rows(alpha) * acc + _values_times_probs(vt_ref, blk, p.astype(BF16))
    return m_new, acc


def _attend(i, own_scores, scores, bias_row, vt_ref, s_refs, o_ref):
    s_a, s_b = s_refs
    last = jnp.maximum(i - 1, 0)
    s_a[...] = own_scores
    s_b[...] = scores(0)
    state = _softmax_first(s_a[...], vt_ref, i)
    s_a[...] = scores(jnp.minimum(1, last))
    state = _softmax_step(s_b[...], bias_row(0), vt_ref, 0, state)

    def two_blocks(t, state):
        j0 = 2 * t + 1
        j1 = j0 + 1
        s_b[...] = scores(jnp.minimum(j1, last))
        state = _softmax_step(s_a[...], bias_row(j0), vt_ref, j0, state)
        s_a[...] = scores(jnp.minimum(j1 + 1, last))
        return _softmax_step(s_b[...], bias_row(j1), vt_ref, j1, state)

    _, acc = lax.fori_loop(0, i // 2, two_blocks, state)
    heads = []
    for h in range(N_HEADS):
        r0 = h * HEAD_ROWS
        heads.append(acc[r0:r0 + HEAD_DIM] / acc[r0 + HEAD_DIM:r0 + HEAD_DIM + 1])
    o_ref[...] = jnp.concatenate(heads, axis=0).T.astype(o_ref.dtype)


def _moba_kernel(z_ref, o_ref, kmt_hi_ref, kmt_lo_ref, vt_ref, bias_ref, sa_ref, sb_ref):
    i = pl.program_id(1)

    @pl.when(i == 0)
    def _prepare_batch():
        means = []
        for j in range(N_ATT_BLOCKS):
            kj = z_ref[j * ATT_BLOCK:(j + 1) * ATT_BLOCK, W_MIX:2 * W_MIX].astype(F32)
            means.append(jnp.sum(kj, axis=0, keepdims=True) * (1.0 / ATT_BLOCK))
        mt = jnp.concatenate(means * N_HEADS, axis=0)
        r = lax.broadcasted_iota(jnp.int32, mt.shape, 0)
        c = lax.broadcasted_iota(jnp.int32, mt.shape, 1)
        hi, lo = _split_bf16(jnp.where((c >> 6) == (r >> 3), mt, 0.0))
        kmt_hi_ref[...] = hi
        kmt_lo_ref[...] = lo
        _store_transposed_v(z_ref, vt_ref)

    row0 = pl.multiple_of(i * ATT_BLOCK, ATT_BLOCK)
    q = z_ref[pl.ds(row0, ATT_BLOCK), 0:W_MIX]
    q_stack = _stacked_queries(z_ref, row0)

    def scores(blk):
        r0 = pl.multiple_of(blk * ATT_BLOCK, ATT_BLOCK)
        return _dot_nt(z_ref[pl.ds(r0, ATT_BLOCK), W_MIX:2 * W_MIX], q_stack)

    gates = _dot_nt(kmt_hi_ref[...], q) + _dot_nt(kmt_lo_ref[...], q)
    blk = lax.broadcasted_iota(jnp.int32, (N_ATT_BLOCKS, ATT_BLOCK), 0)
    past = blk < i
    biases = []
    for h in range(N_HEADS):
        g = jnp.where(past, gates[h * N_ATT_BLOCKS:(h + 1) * N_ATT_BLOCKS], -jnp.inf)
        rank = jnp.zeros(g.shape, F32)
        for d in range(1, N_ATT_BLOCKS):
            lower = pltpu.roll(g, d, axis=0)
            rank = rank + jnp.where((blk >= d) & (lower >= g), 1.0, 0.0)
            upper = pltpu.roll(g, N_ATT_BLOCKS - d, axis=0)
            rank = rank + jnp.where((blk + d < N_ATT_BLOCKS) & (upper > g), 1.0, 0.0)
        biases.append(jnp.where(past & (rank < MOBA_TOPK), 0.0, MASK_BIAS))
    bias = jnp.concatenate(biases, axis=1)
    for j in range(N_ATT_BLOCKS):
        bias_ref[j] = bias[j:j + 1]

    ki = lax.broadcasted_iota(jnp.int32, (ATT_BLOCK, N_HEADS * ATT_BLOCK), 0)
    qi = lax.broadcasted_iota(jnp.int32, (ATT_BLOCK, N_HEADS * ATT_BLOCK), 1) & (ATT_BLOCK - 1)
    own = jnp.where(ki <= qi, scores(i), -jnp.inf)
    _attend(i, own, scores, lambda j: bias_ref[j], vt_ref, (sa_ref, sb_ref), o_ref)


def _moba(za):
    za3 = za.reshape(BATCH, SEQ, CUT_A)
    out = pl.pallas_call(
        _moba_kernel,
        grid=(BATCH, N_ATT_BLOCKS),
        in_specs=[pl.BlockSpec((None, SEQ, CUT_A), lambda b, i: (b, 0, 0))],
        out_specs=pl.BlockSpec((None, ATT_BLOCK, W_MIX), lambda b, i: (b, i, 0)),
        out_shape=jax.ShapeDtypeStruct((BATCH, SEQ, W_MIX), BF16),
        scratch_shapes=[
            pltpu.VMEM((N_HEADS * N_ATT_BLOCKS, W_MIX), BF16),
            pltpu.VMEM((N_HEADS * N_ATT_BLOCKS, W_MIX), BF16),
            pltpu.VMEM((N_ATT_BLOCKS, N_HEADS * HEAD_ROWS, ATT_BLOCK), BF16),
            pltpu.VMEM((N_ATT_BLOCKS, 1, N_HEADS * ATT_BLOCK), F32),
            pltpu.VMEM((ATT_BLOCK, N_HEADS * ATT_BLOCK), F32),
            pltpu.VMEM((ATT_BLOCK, N_HEADS * ATT_BLOCK), F32),
        ],
        compiler_params=pltpu.CompilerParams(
            dimension_semantics=("parallel", "arbitrary"), vmem_limit_bytes=VMEM_LIMIT),
        name="moba_attention",
    )(za3)
    return out.reshape(N_TOK, W_MIX)


def _dilated_bias_table():
    d = np.arange(ATT_BLOCK)[None, :] - np.arange(ATT_BLOCK)[:, None]
    tiles = []
    for k in range(N_ATT_BLOCKS):
        dist = d + k * ATT_BLOCK
        count = np.zeros(dist.shape, np.float64)
        for window, dil in DILATED_PATTERNS:
            count += (dist >= 0) & (dist <= window) & (dist % dil == 0)
        with np.errstate(divide="ignore"):
            tiles.append(np.log(count))
    return jnp.asarray(np.stack(tiles), F32)


def _dilated_kernel(z_ref, bias_ref, o_ref, vt_ref, sa_ref, sb_ref):
    i = pl.program_id(1)

    @pl.when(i == 0)
    def _prepare_batch():
        _store_transposed_v(z_ref, vt_ref)

    row0 = pl.multiple_of(i * ATT_BLOCK, ATT_BLOCK)
    q_stack = _stacked_queries(z_ref, row0)

    def scores(blk):
        r0 = pl.multiple_of(blk * ATT_BLOCK, ATT_BLOCK)
        s = _dot_nt(z_ref[pl.ds(r0, ATT_BLOCK), W_MIX:2 * W_MIX], q_stack)
        b = bias_ref[i - blk]
        return jnp.concatenate([s[:, _head_cols(h)] + b for h in range(N_HEADS)], axis=1)

    def bias_row(j):
        return jnp.full((1, N_HEADS * ATT_BLOCK), jnp.where(j < i, 0.0, MASK_BIAS), F32)

    _attend(i, scores(i), scores, bias_row, vt_ref, (sa_ref, sb_ref), o_ref)


def _dilated(zd, bias_table):
    zd3 = zd.reshape(BATCH, SEQ, 3 * W_MIX)
    out = pl.pallas_call(
        _dilated_kernel,
        grid=(BATCH, N_ATT_BLOCKS),
        in_specs=[
            pl.BlockSpec((None, SEQ, 3 * W_MIX), lambda b, i: (b, 0, 0)),
            pl.BlockSpec((N_ATT_BLOCKS, ATT_BLOCK, ATT_BLOCK), lambda b, i: (0, 0, 0)),
        ],
        out_specs=pl.BlockSpec((None, ATT_BLOCK, W_MIX), lambda b, i: (b, i, 0)),
        out_shape=jax.ShapeDtypeStruct((BATCH, SEQ, W_MIX), BF16),
        scratch_shapes=[pltpu.VMEM((N_ATT_BLOCKS, N_HEADS * HEAD_ROWS, ATT_BLOCK), BF16),
                        pltpu.VMEM((ATT_BLOCK, N_HEADS * ATT_BLOCK), F32),
                        pltpu.VMEM((ATT_BLOCK, N_HEADS * ATT_BLOCK), F32)],
        compiler_params=pltpu.CompilerParams(
            dimension_semantics=("parallel", "arbitrary"), vmem_limit_bytes=VMEM_LIMIT),
        name="dilated_attention",
    )(zd3, bias_table)
    return out.reshape(N_TOK, W_MIX)


def _gelu_tanh(x):
    return 0.5 * x * (1.0 + jnp.tanh(np.sqrt(2.0 / np.pi).astype(np.float32) * (x + 0.044715 * (x * x * x))))


def _sgu_conv_kernel(zb_ref, zc_ref, lng_ref, ws_ref, bs_ref, cw_ref, yb_ref, yc_ref):
    r = lax.broadcasted_iota(jnp.int32, (N_HEADS * SGU_CHUNK, SGU_CHUNK), 0)
    c = lax.broadcasted_iota(jnp.int32, (N_HEADS * SGU_CHUNK, SGU_CHUNK), 1)
    w_stack = jnp.where((r & (SGU_CHUNK - 1)) >= c, ws_ref[...], 0.0).astype(BF16)
    group = lax.broadcasted_iota(jnp.int32, (SGU_CHUNK, W_MIX), 1) >> 6
    ln_g = lng_ref[...]
    b_tile = bs_ref[...]

    def chunk(n, carry):
        r0 = pl.multiple_of(n * SGU_CHUNK, SGU_CHUNK)
        gz = _gelu_tanh(zb_ref[pl.ds(r0, SGU_CHUNK), :].astype(F32))
        u = gz[:, 0:W_MIX]
        v = gz[:, W_MIX:2 * W_MIX]
        vc = v - jnp.mean(v, axis=-1, keepdims=True)
        vn = vc * lax.rsqrt(jnp.mean(vc * vc, axis=-1, keepdims=True) + LN_EPS) * ln_g
        mixed = _dot(w_stack, vn.astype(BF16))
        sv = mixed[0:SGU_CHUNK]
        for gi in range(1, N_HEADS):
            sv = jnp.where(group == gi, mixed[gi * SGU_CHUNK:(gi + 1) * SGU_CHUNK], sv)
        yb_ref[pl.ds(r0, SGU_CHUNK), :] = (u * (sv + b_tile)).astype(yb_ref.dtype)
        return carry

    lax.fori_loop(0, SEQ // SGU_CHUNK, chunk, 0)

    zc = zc_ref[...].astype(F32)
    z = zc[:, W_MIX:2 * W_MIX] * zc[:, 2 * W_MIX:3 * W_MIX]
    t = lax.broadcasted_iota(jnp.int32, (SEQ, W_MIX), 0)
    z1 = jnp.where(t >= 1, pltpu.roll(z, 1, axis=0), 0.0)
    z2 = jnp.where(t >= 2, pltpu.roll(z, 2, axis=0), 0.0)
    y = z2 * cw_ref[0:1, :]
    y = y + z1 * cw_ref[1:2, :]
    y = y + z * cw_ref[2:3, :]
    yc_ref[...] = (zc[:, 0:W_MIX] * y).astype(yc_ref.dtype)


def _sgu_conv(zb, zc, ln_g, w_s, b_s, conv_w):
    w_stack = w_s.reshape(N_HEADS * SGU_CHUNK, SGU_CHUNK)
    b_tile = jnp.repeat(b_s.T, HEAD_DIM, axis=1)
    yb, yc = pl.pallas_call(
        _sgu_conv_kernel,
        grid=(BATCH,),
        in_specs=[
            pl.BlockSpec((None, SEQ, 2 * W_MIX), lambda b: (b, 0, 0)),
            pl.BlockSpec((None, SEQ, 3 * W_MIX), lambda b: (b, 0, 0)),
            pl.BlockSpec((1, W_MIX), lambda b: (0, 0)),
            pl.BlockSpec((N_HEADS * SGU_CHUNK, SGU_CHUNK), lambda b: (0, 0)),
            pl.BlockSpec((SGU_CHUNK, W_MIX), lambda b: (0, 0)),
            pl.BlockSpec((3, W_MIX), lambda b: (0, 0)),
        ],
        out_specs=[pl.BlockSpec((None, SEQ, W_MIX), lambda b: (b, 0, 0))] * 2,
        out_shape=[jax.ShapeDtypeStruct((BATCH, SEQ, W_MIX), BF16)] * 2,
        compiler_params=pltpu.CompilerParams(
            dimension_semantics=("parallel",), vmem_limit_bytes=VMEM_LIMIT),
        name="sgu_conv",
    )(zb.reshape(BATCH, SEQ, 2 * W_MIX), zc.reshape(BATCH, SEQ, 3 * W_MIX),
      ln_g.reshape(1, W_MIX), w_stack, b_tile, conv_w)
    return yb.reshape(N_TOK, W_MIX), yc.reshape(N_TOK, W_MIX)


def _outproj_residual(x_ref, ya_ref, yb_ref, yc_ref, yd_ref, w_ref):
    acc = _dot(ya_ref[...], w_ref[0:W_MIX, :])
    acc = acc + _dot(yb_ref[...], w_ref[W_MIX:2 * W_MIX, :])
    acc = acc + _dot(yc_ref[...], w_ref[2 * W_MIX:3 * W_MIX, :])
    acc = acc + _dot(yd_ref[...], w_ref[3 * W_MIX:4 * W_MIX, :])
    return x_ref[...] + acc


def _outproj_dense_kernel(x_ref, ya_ref, yb_ref, yc_ref, yd_ref, w_ref, g_ref, x1_ref, h_ref):
    x1 = _outproj_residual(x_ref, ya_ref, yb_ref, yc_ref, yd_ref, w_ref)
    x1_ref[...] = x1
    h_ref[...] = _rms(x1, g_ref[...]).astype(BF16)


_R_E1, _R_E2, _R_G1, _R_G2, _R_RANK1, _R_RANK2 = range(6)


def _outproj_router_kernel(x_ref, ya_ref, yb_ref, yc_ref, yd_ref, w_ref, g_ref, wr_hi_ref, wr_lo_ref,
                           x1_ref, h_ref, route_ref, counts_ref, running_ref):
    step = pl.program_id(0)

    @pl.when(step == 0)
    def _():
        running_ref[...] = jnp.zeros_like(running_ref)

    x1 = _outproj_residual(x_ref, ya_ref, yb_ref, yc_ref, yd_ref, w_ref)
    x1_ref[...] = x1
    h = _rms(x1, g_ref[...])
    h_ref[...] = h

    h_hi, h_lo = _split_bf16(h)
    logits = _dot(h_hi, wr_hi_ref[...]) + (_dot(h_lo, wr_hi_ref[...]) + _dot(h_hi, wr_lo_ref[...]))
    lane = lax.broadcasted_iota(jnp.int32, (ROW_TILE, LANES), 1)
    lane_f = lane.astype(F32)
    logits = jnp.where(lane < N_EXPERTS, logits, -jnp.inf)
    m1 = jnp.max(logits, axis=-1, keepdims=True)
    e1 = jnp.min(jnp.where(logits == m1, lane_f, float(LANES)), axis=-1, keepdims=True)
    rest = jnp.where(lane_f == e1, -jnp.inf, logits)
    m2 = jnp.max(rest, axis=-1, keepdims=True)
    e2 = jnp.min(jnp.where(rest == m2, lane_f, float(LANES)), axis=-1, keepdims=True)
    t = jnp.exp(m2 - m1)
    g1 = 1.0 / (1.0 + t)
    g2 = t / (1.0 + t)

    chosen = jnp.where((lane_f == e1) | (lane_f == e2), 1.0, 0.0)
    ri = lax.broadcasted_iota(jnp.int32, (ROW_TILE, ROW_TILE), 0)
    ci = lax.broadcasted_iota(jnp.int32, (ROW_TILE, ROW_TILE), 1)
    earlier = jnp.where(ci < ri, 1.0, 0.0).astype(BF16)
    before = _dot(earlier, chosen.astype(BF16)) + running_ref[...]
    rank1 = jnp.sum(jnp.where(lane_f == e1, before, 0.0), axis=-1, keepdims=True)
    rank2 = jnp.sum(jnp.where(lane_f == e2, before, 0.0), axis=-1, keepdims=True)
    running_ref[...] = running_ref[...] + jnp.sum(chosen, axis=0, keepdims=True)
    counts_ref[...] = running_ref[...]

    rec = jnp.zeros((ROW_TILE, LANES), F32)
    for pos, val in ((_R_E1, e1), (_R_E2, e2), (_R_G1, g1), (_R_G2, g2),
                     (_R_RANK1, rank1), (_R_RANK2, rank2)):
        rec = jnp.where(lane == pos, val, rec)
    route_ref[...] = rec


def _outproj(x, ys, w, g, router=None):
    row = lambda width: pl.BlockSpec((ROW_TILE, width), lambda i: (i, 0))
    const = lambda shape: pl.BlockSpec(shape, lambda i: (0, 0))
    in_specs = [row(D_MODEL)] + [row(W_MIX)] * 4 + [const((D_MODEL, D_MODEL)), const((1, D_MODEL))]
    if router is None:
        return pl.pallas_call(
            _outproj_dense_kernel,
            grid=(N_TOK // ROW_TILE,),
            in_specs=in_specs,
            out_specs=[row(D_MODEL), row(D_MODEL)],
            out_shape=[jax.ShapeDtypeStruct((N_TOK, D_MODEL), F32),
                       jax.ShapeDtypeStruct((N_TOK, D_MODEL), BF16)],
            compiler_params=pltpu.CompilerParams(
                dimension_semantics=("parallel",), vmem_limit_bytes=VMEM_LIMIT),
            name="outproj_norm",
        )(x, *ys, w, g)
    wr_hi, wr_lo = router
    return pl.pallas_call(
        _outproj_router_kernel,
        grid=(N_TOK // ROW_TILE,),
        in_specs=in_specs + [const((D_MODEL, LANES)), const((D_MODEL, LANES))],
        out_specs=[row(D_MODEL), row(D_MODEL), row(LANES), const((1, LANES))],
        out_shape=[jax.ShapeDtypeStruct((N_TOK, D_MODEL), F32),
                   jax.ShapeDtypeStruct((N_TOK, D_MODEL), F32),
                   jax.ShapeDtypeStruct((N_TOK, LANES), F32),
                   jax.ShapeDtypeStruct((1, LANES), F32)],
        scratch_shapes=[pltpu.VMEM((1, LANES), F32)],
        compiler_params=pltpu.CompilerParams(
            dimension_semantics=("arbitrary",), vmem_limit_bytes=VMEM_LIMIT),
        name="outproj_router",
    )(x, *ys, w, g, wr_hi, wr_lo)


def _swiglu_rows(h, wg_ref, wu_ref, wd_ref, chunk=FF_CHUNK, copies=()):
    copies = list(copies)
    n_gaps = 3 * (D_FF // chunk)
    per_gap = -(-len(copies) // n_gaps)

    def start_some():
        for c in copies[:per_gap]:
            c.start()
        del copies[:per_gap]

    acc = None
    for c0 in range(0, D_FF, chunk):
        gate = _dot(h, wg_ref[:, c0:c0 + chunk])
        start_some()
        up = _dot(h, wu_ref[:, c0:c0 + chunk])
        start_some()
        act = (gate * (1.0 / (1.0 + jnp.exp(-gate))) * up).astype(BF16)
        part = _dot(act, wd_ref[c0:c0 + chunk, :])
        start_some()
        acc = part if acc is None else acc + part
    return acc


def _dense_ffn_kernel(x_ref, h_ref, wg_ref, wu_ref, wd_ref, o_ref):
    o_ref[...] = x_ref[...] + _swiglu_rows(h_ref[...], wg_ref, wu_ref, wd_ref)


def _dense_ffn(x1, h, wg, wu, wd):
    row = lambda: pl.BlockSpec((ROW_TILE, D_MODEL), lambda i: (i, 0))
    once = pl.Buffered(1)
    return pl.pallas_call(
        _dense_ffn_kernel,
        grid=(N_TOK // ROW_TILE,),
        in_specs=[
            row(), row(),
            pl.BlockSpec((D_MODEL, D_FF), lambda i: (0, 0), pipeline_mode=once),
            pl.BlockSpec((D_MODEL, D_FF), lambda i: (0, 0), pipeline_mode=once),
            pl.BlockSpec((D_FF, D_MODEL), lambda i: (0, 0), pipeline_mode=once),
        ],
        out_specs=row(),
        out_shape=jax.ShapeDtypeStruct((N_TOK, D_MODEL), F32),
        compiler_params=pltpu.CompilerParams(
            dimension_semantics=("parallel",), vmem_limit_bytes=VMEM_LIMIT),
        name="dense_swiglu",
    )(x1, h, wg, wu, wd)


def _row_copy(src_ref, src_row, dst_ref, dst_row, sem):
    return pltpu.make_async_copy(src_ref.at[pl.ds(src_row, 1), :], dst_ref.at[pl.ds(dst_row, 1), :], sem)


def _invert_kernel(dest_ref, pad_lo_ref, pad_hi_ref, src_row_ref, dst_row_ref):
    for e in range(N_EXPERTS + 1):
        def mark_pad(row, carry):
            src_row_ref[row] = 0
            dst_row_ref[row] = N_TOK + (row & (2 * MOE_BLOCK - 1))
            return carry

        lax.fori_loop(pad_lo_ref[e], pad_hi_ref[e], mark_pad, 0)

    def place(s, carry):
        row = dest_ref[s]
        src_row_ref[row] = s >> 1
        dst_row_ref[row] = (s & 1) * Y_PLANE + (s >> 1)
        return carry

    lax.fori_loop(0, 2 * N_TOK, place, 0, unroll=8)


def _invert(dest_flat, pad_lo, pad_hi):
    return pl.pallas_call(
        _invert_kernel,
        grid_spec=pltpu.PrefetchScalarGridSpec(
            num_scalar_prefetch=3,
            grid=(1,),
            in_specs=[],
            out_specs=[pl.BlockSpec(memory_space=pltpu.SMEM)] * 2,
        ),
        out_shape=[jax.ShapeDtypeStruct((MOE_CAP,), jnp.int32)] * 2,
        name="moe_invert",
    )(dest_flat, pad_lo, pad_hi)


def _expert_kernel(block_e_ref, src_row_ref, dst_row_ref, h_ref, wg_ref, wu_ref, wd_ref, y_ref,
                   x_buf, y_buf, gather_sem, scatter_sem):
    del block_e_ref
    i = pl.program_id(0)
    last = pl.num_programs(0) - 1
    cur = i & 1
    nxt = 1 - cur

    def gather(block, buf):
        return [_row_copy(h_ref, src_row_ref[block * MOE_BLOCK + r], x_buf.at[buf], r, gather_sem.at[buf])
                for r in range(MOE_BLOCK)]

    def scatter(block, buf):
        return [_row_copy(y_buf.at[buf], r, y_ref, dst_row_ref[block * MOE_BLOCK + r], scatter_sem.at[buf])
                for r in range(MOE_BLOCK)]

    prev = jnp.maximum(i - 1, 0)

    @pl.when(i == 0)
    def _():
        y_buf[...] = jnp.zeros_like(y_buf)
        for c in gather(0, 0):
            c.start()
        for b in range(2):
            dump = pltpu.make_async_copy(
                y_buf.at[b], y_ref.at[pl.ds(N_TOK + b * MOE_BLOCK, MOE_BLOCK), :], scatter_sem.at[b])
            dump.start()
            dump.wait()

    for c in gather(i, cur):
        c.wait()

    @pl.when(i >= 1)
    def _():
        for c in scatter(jnp.maximum(i - 2, 0), cur):
            c.wait()

    copies = gather(jnp.minimum(i + 1, last), nxt) + scatter(prev, nxt)
    y_buf[cur] = _swiglu_rows(x_buf[cur].astype(BF16), wg_ref, wu_ref, wd_ref,
                              chunk=EXPERT_FF_CHUNK, copies=copies)

    @pl.when(i == last)
    def _():
        for c in scatter(last, cur):
            c.start()
        for c in gather(last, nxt):
            c.wait()
        for c in scatter(prev, nxt):
            c.wait()
        for c in scatter(last, cur):
            c.wait()


def _experts(block_e, src_row, dst_row, h, wg, wu, wd):
    weights = lambda shape: pl.BlockSpec((None,) + shape, lambda i, be, sr, dr: (be[i], 0, 0))
    return pl.pallas_call(
        _expert_kernel,
        grid_spec=pltpu.PrefetchScalarGridSpec(
            num_scalar_prefetch=3,
            grid=(N_MOE_BLOCKS,),
            in_specs=[pl.BlockSpec(memory_space=pl.ANY),
                      weights((D_MODEL, D_FF)), weights((D_MODEL, D_FF)), weights((D_FF, D_MODEL))],
            out_specs=pl.BlockSpec(memory_space=pl.ANY),
            scratch_shapes=[pltpu.VMEM((2, MOE_BLOCK, D_MODEL), F32),
                            pltpu.VMEM((2, MOE_BLOCK, D_MODEL), F32),
                            pltpu.SemaphoreType.DMA((2,)),
                            pltpu.SemaphoreType.DMA((2,))],
        ),
        out_shape=jax.ShapeDtypeStruct((Y_PLANE + N_TOK, D_MODEL), F32),
        compiler_params=pltpu.CompilerParams(
            dimension_semantics=("arbitrary",), vmem_limit_bytes=VMEM_LIMIT),
        name="moe_experts",
    )(block_e, src_row, dst_row, h, wg, wu, wd)


def _combine_kernel(x_ref, route_ref, g_ref, y1_ref, y2_ref, o_ref):
    route = route_ref[...]
    g1 = route[:, _R_G1:_R_G1 + 1]
    g2 = route[:, _R_G2:_R_G2 + 1]
    x2 = x_ref[...] + (y1_ref[...] * g1 + y2_ref[...] * g2)
    o_ref[...] = _rms(x2, g_ref[...])


def _combine_final_norm(x1, route, g_final, y):
    row = lambda width: pl.BlockSpec((ROW_TILE, width), lambda i: (i, 0))
    return pl.pallas_call(
        _combine_kernel,
        grid=(N_TOK // ROW_TILE,),
        in_specs=[row(D_MODEL), row(LANES), pl.BlockSpec((1, D_MODEL), lambda i: (0, 0)),
                  row(D_MODEL),
                  pl.BlockSpec((ROW_TILE, D_MODEL), lambda i: (Y_PLANE // ROW_TILE + i, 0))],
        out_specs=row(D_MODEL),
        out_shape=jax.ShapeDtypeStruct((N_TOK, D_MODEL), F32),
        compiler_params=pltpu.CompilerParams(
            dimension_semantics=("parallel",), vmem_limit_bytes=VMEM_LIMIT),
        name="moe_combine_norm",
    )(x1, route, g_final, y, y)


def _final_norm_kernel(x_ref, g_ref, o_ref):
    o_ref[...] = _rms(x_ref[...], g_ref[...])


def _final_norm(x, g):
    return pl.pallas_call(
        _final_norm_kernel,
        grid=(N_TOK // ROW_TILE,),
        in_specs=[pl.BlockSpec((ROW_TILE, D_MODEL), lambda i: (i, 0)),
                  pl.BlockSpec((1, D_MODEL), lambda i: (0, 0))],
        out_specs=pl.BlockSpec((ROW_TILE, D_MODEL), lambda i: (i, 0)),
        out_shape=jax.ShapeDtypeStruct((N_TOK, D_MODEL), F32),
        compiler_params=pltpu.CompilerParams(dimension_semantics=("parallel",)),
        name="final_norm",
    )(x, g)


def _expert_row_plan(route, counts):
    experts = route[:, _R_E1:_R_E2 + 1].astype(jnp.int32)
    ranks = route[:, _R_RANK1:_R_RANK2 + 1].astype(jnp.int32)
    counts = counts[0, :N_EXPERTS].astype(jnp.int32)
    padded = ((counts + MOE_BLOCK - 1) // MOE_BLOCK) * MOE_BLOCK
    ends = jnp.cumsum(padded)
    starts = ends - padded
    start_of = jnp.sum(jnp.where(experts[..., None] == jnp.arange(N_EXPERTS), starts, 0), axis=-1)
    dest = (start_of + ranks).reshape(-1)
    block_row = jnp.arange(N_MOE_BLOCKS, dtype=jnp.int32) * MOE_BLOCK
    block_e = jnp.minimum(jnp.sum(block_row[:, None] >= ends[None, :], axis=-1), N_EXPERTS - 1)
    pad_lo = jnp.concatenate([starts + counts, ends[-1:]])
    pad_hi = jnp.concatenate([ends, jnp.full((1,), MOE_CAP, jnp.int32)])
    return dest.astype(jnp.int32), block_e.astype(jnp.int32), pad_lo.astype(jnp.int32), pad_hi.astype(jnp.int32)


def kernel(x, g_mix, w_in, sgu_ln_g, sgu_w, sgu_b, conv_w, w_out, g_ffn, dense_w_gate, dense_w_up,
           dense_w_down, router_w, moe_w_gate, moe_w_up, moe_w_down, g_final):
    x = x.reshape(N_TOK, D_MODEL)
    bias_table = _dilated_bias_table()
    out = None
    for l in range(DEPTH):
        za, zb, zc, zd = _norm_inproj(x, g_mix[l].reshape(1, D_MODEL), w_in[l].astype(BF16))
        ya = _moba(za)
        yd = _dilated(zd, bias_table)
        yb, yc = _sgu_conv(zb, zc, sgu_ln_g[l], sgu_w[l], sgu_b[l], conv_w[l])
        ys = (ya, yb, yc, yd)
        g2 = g_ffn[l].reshape(1, D_MODEL)
        j = l // 2
        if l % 2 == 0:
            x1, h = _outproj(x, ys, w_out[l].astype(BF16), g2)
            x = _dense_ffn(x1, h, dense_w_gate[j].astype(BF16), dense_w_up[j].astype(BF16),
                           dense_w_down[j].astype(BF16))
            if l == DEPTH - 1:
                out = _final_norm(x, g_final.reshape(1, D_MODEL))
        else:
            wr = jnp.pad(router_w[j], ((0, 0), (0, LANES - N_EXPERTS)))
            x1, h, route, counts = _outproj(x, ys, w_out[l].astype(BF16), g2, router=_split_bf16(wr))
            dest, block_e, pad_lo, pad_hi = _expert_row_plan(route, counts)
            src_row, dst_row = _invert(dest, pad_lo, pad_hi)
            y = _experts(block_e, src_row, dst_row, h, moe_w_gate[j].astype(BF16), moe_w_up[j].astype(BF16),
                         moe_w_down[j].astype(BF16))
            if l == DEPTH - 1:
                out = _combine_final_norm(x1, route, g_final.reshape(1, D_MODEL), y)
            else:
                raise NotImplementedError("an expert layer that is not the last layer")
    return out.reshape(BATCH, SEQ, D_MODEL)
```

```python
import functools

import numpy as np
import jax
import jax.numpy as jnp
from jax import lax
from jax.experimental import pallas as pl
from jax.experimental.pallas import tpu as pltpu

D_MODEL = 1024
BATCH = 8
SEQ = 2048
DEPTH = 2
N_TOK = BATCH * SEQ
HEAD_DIM = 64
N_HEADS = 4
W_MIX = N_HEADS * HEAD_DIM
CUT_A = 3 * W_MIX
CUT_B = CUT_A + 2 * W_MIX
CUT_C = CUT_B + 3 * W_MIX
IN_COLS = CUT_C + 3 * W_MIX
ATT_BLOCK = 256
HEAD_ROWS = HEAD_DIM + 16
N_ATT_BLOCKS = SEQ // ATT_BLOCK
MOBA_TOPK = 3
SGU_CHUNK = 128
DILATED_PATTERNS = ((128, 1), (512, 4), (2048, 16))
D_FF = 2816
FF_CHUNK = D_FF // 2
N_EXPERTS = 8
MOE_BLOCK = 256
N_MOE_BLOCKS = (N_TOK * 2) // MOE_BLOCK + N_EXPERTS
MOE_CAP = N_MOE_BLOCKS * MOE_BLOCK
Y_PLANE = N_TOK + 2 * MOE_BLOCK
RMS_EPS = 1e-6
LN_EPS = 1e-5
ATTN_SCALE = HEAD_DIM ** -0.5
MASK_BIAS = -1e30

LANES = 128
MXU_TILE = 256
ROW_TILE = 512
VMEM_LIMIT = 56 * 1024 * 1024

F32 = jnp.float32
BF16 = jnp.bfloat16


def _dot(a, b):
    return jnp.dot(a, b, preferred_element_type=F32)


def _dot_nt(a, b):
    return lax.dot_general(a, b, (((1,), (1,)), ((), ())), preferred_element_type=F32)


def _rms(x, g):
    return x * lax.rsqrt(jnp.mean(x * x, axis=-1, keepdims=True) + RMS_EPS) * g


def _split_bf16(x):
    hi = x.astype(BF16)
    lo = (x - hi.astype(F32)).astype(BF16)
    return hi, lo


def _norm_inproj_kernel(x_ref, g_ref, w_ref, za_ref, zb_ref, zc_ref, zd_ref):
    h = _rms(x_ref[...], g_ref[...]).astype(BF16)
    za_ref[...] = _dot(h, w_ref[:, 0:CUT_A]).astype(BF16)
    zb_ref[...] = _dot(h, w_ref[:, CUT_A:CUT_B]).astype(BF16)
    zc_ref[...] = _dot(h, w_ref[:, CUT_B:CUT_C]).astype(BF16)
    zd_ref[...] = _dot(h, w_ref[:, CUT_C:IN_COLS]).astype(BF16)


def _norm_inproj(x, g, w):
    widths = (CUT_A, CUT_B - CUT_A, CUT_C - CUT_B, IN_COLS - CUT_C)
    return pl.pallas_call(
        _norm_inproj_kernel,
        grid=(N_TOK // ROW_TILE,),
        in_specs=[
            pl.BlockSpec((ROW_TILE, D_MODEL), lambda i: (i, 0)),
            pl.BlockSpec((1, D_MODEL), lambda i: (0, 0)),
            pl.BlockSpec((D_MODEL, IN_COLS), lambda i: (0, 0)),
        ],
        out_specs=[pl.BlockSpec((ROW_TILE, w_), lambda i: (i, 0)) for w_ in widths],
        out_shape=[jax.ShapeDtypeStruct((N_TOK, w_), BF16) for w_ in widths],
        compiler_params=pltpu.CompilerParams(
            dimension_semantics=("parallel",), vmem_limit_bytes=VMEM_LIMIT),
        name="norm_inproj",
    )(x, g, w)


def _head_lane_mask(h, width):
    lane = lax.broadcasted_iota(jnp.int32, (ATT_BLOCK, width), 1)
    return (lane >= h * HEAD_DIM) & (lane < (h + 1) * HEAD_DIM)


def _store_transposed_v(z_ref, vt_ref):
    row = lax.broadcasted_iota(jnp.int32, (HEAD_ROWS - HEAD_DIM, ATT_BLOCK), 0)
    ones_row = jnp.where(row == 0, 1.0, 0.0).astype(BF16)
    for j in range(N_ATT_BLOCKS):
        vt = z_ref[j * ATT_BLOCK:(j + 1) * ATT_BLOCK, 2 * W_MIX:3 * W_MIX].astype(F32).T
        for h in range(N_HEADS):
            vt_ref[j, h * HEAD_ROWS:h * HEAD_ROWS + HEAD_DIM, :] = (
                vt[h * HEAD_DIM:(h + 1) * HEAD_DIM].astype(BF16))
            vt_ref[j, h * HEAD_ROWS + HEAD_DIM:(h + 1) * HEAD_ROWS, :] = ones_row


def _stacked_queries(z_ref, row0):
    qs = z_ref[pl.ds(row0, ATT_BLOCK), 0:W_MIX].astype(F32) * ATTN_SCALE
    return jnp.concatenate(
        [jnp.where(_head_lane_mask(h, W_MIX), qs, 0.0) for h in range(N_HEADS)], axis=0).astype(BF16)


def _head_cols(h):
    return slice(h * ATT_BLOCK, (h + 1) * ATT_BLOCK)


def _values_times_probs(vt_ref, blk, p):
    return jnp.concatenate(
        [_dot(vt_ref[blk, h * HEAD_ROWS:(h + 1) * HEAD_ROWS, :], p[:, _head_cols(h)]) for h in range(N_HEADS)],
        axis=0)


def _per_head_rows(a):
    return jnp.concatenate(
        [jnp.broadcast_to(a[:, _head_cols(h)], (HEAD_ROWS, ATT_BLOCK)) for h in range(N_HEADS)], axis=0)


def _softmax_first(s, vt_ref, blk):
    m = jnp.max(s, axis=0, keepdims=True)
    return m, _values_times_probs(vt_ref, blk, jnp.exp(s - m).astype(BF16))


def _softmax_step(s, bias_row, vt_ref, blk, state):
    m, acc = state
    m_new = jnp.maximum(m, jnp.max(s, axis=0, keepdims=True) + bias_row)
    alpha = jnp.exp(m - m_new)
    p = jnp.exp(s - (m_new - bias_row))
    acc = _per_head_rows(alpha) * acc + _values_times_probs(vt_ref, blk, p.astype(BF16))
    return m_new, acc


def _attend(i, own_scores, scores, bias_row, vt_ref, s_refs, o_ref):
    s_a, s_b = s_refs
    last = jnp.maximum(i - 1, 0)
    s_a[...] = own_scores
    s_b[...] = scores(0)
    state = _softmax_first(s_a[...], vt_ref, i)
    s_a[...] = scores(jnp.minimum(1, last))
    state = _softmax_step(s_b[...], bias_row(0), vt_ref, 0, state)

    def two_blocks(t, state):
        j0 = 2 * t + 1
        j1 = j0 + 1
        s_b[...] = scores(jnp.minimum(j1, last))
        state = _softmax_step(s_a[...], bias_row(j0), vt_ref, j0, state)
        s_a[...] = scores(jnp.minimum(j1 + 1, last))
        return _softmax_step(s_b[...], bias_row(j1), vt_ref, j1, state)

    _, acc = lax.fori_loop(0, i // 2, two_blocks, state)
    heads = []
    for h in range(N_HEADS):
        r0 = h * HEAD_ROWS
        heads.append(acc[r0:r0 + HEAD_DIM] / acc[r0 + HEAD_DIM:r0 + HEAD_DIM + 1])
    o_ref[...] = jnp.concatenate(heads, axis=0).T.astype(o_ref.dtype)


def _moba_kernel(z_ref, o_ref, kmt_hi_ref, kmt_lo_ref, vt_ref, bias_ref, sa_ref, sb_ref):
    i = pl.program_id(1)

    @pl.when(i == 0)
    def _prepare_batch():
        means = []
        for j in range(N_ATT_BLOCKS):
            kj = z_ref[j * ATT_BLOCK:(j + 1) * ATT_BLOCK, W_MIX:2 * W_MIX].astype(F32)
            means.append(jnp.sum(kj, axis=0, keepdims=True) * (1.0 / ATT_BLOCK))
        mt = jnp.concatenate(means * N_HEADS, axis=0)
        r = lax.broadcasted_iota(jnp.int32, mt.shape, 0)
        c = lax.broadcasted_iota(jnp.int32, mt.shape, 1)
        hi, lo = _split_bf16(jnp.where((c >> 6) == (r >> 3), mt, 0.0))
        kmt_hi_ref[...] = hi
        kmt_lo_ref[...] = lo
        _store_transposed_v(z_ref, vt_ref)

    row0 = pl.multiple_of(i * ATT_BLOCK, ATT_BLOCK)
    q = z_ref[pl.ds(row0, ATT_BLOCK), 0:W_MIX]
    q_stack = _stacked_queries(z_ref, row0)

    def scores(blk):
        r0 = pl.multiple_of(blk * ATT_BLOCK, ATT_BLOCK)
        return _dot_nt(z_ref[pl.ds(r0, ATT_BLOCK), W_MIX:2 * W_MIX], q_stack)

    gates = _dot_nt(kmt_hi_ref[...], q) + _dot_nt(kmt_lo_ref[...], q)
    blk = lax.broadcasted_iota(jnp.int32, (N_ATT_BLOCKS, ATT_BLOCK), 0)
    past = blk < i
    biases = []
    for h in range(N_HEADS):
        g = jnp.where(past, gates[h * N_ATT_BLOCKS:(h + 1) * N_ATT_BLOCKS], -jnp.inf)
        rank = jnp.zeros(g.shape, F32)
        for d in range(1, N_ATT_BLOCKS):
            lower = pltpu.roll(g, d, axis=0)
            rank = rank + jnp.where((blk >= d) & (lower >= g), 1.0, 0.0)
            upper = pltpu.roll(g, N_ATT_BLOCKS - d, axis=0)
            rank = rank + jnp.where((blk + d < N_ATT_BLOCKS) & (upper > g), 1.0, 0.0)
        biases.append(jnp.where(past & (rank < MOBA_TOPK), 0.0, MASK_BIAS))
    bias = jnp.concatenate(biases, axis=1)
    for j in range(N_ATT_BLOCKS):
        bias_ref[j] = bias[j:j + 1]

    ki = lax.broadcasted_iota(jnp.int32, (ATT_BLOCK, N_HEADS * ATT_BLOCK), 0)
    qi = lax.broadcasted_iota(jnp.int32, (ATT_BLOCK, N_HEADS * ATT_BLOCK), 1) & (ATT_BLOCK - 1)
    own = jnp.where(ki <= qi, scores(i), -jnp.inf)
    _attend(i, own, scores, lambda j: bias_ref[j], vt_ref, (sa_ref, sb_ref), o_ref)


def _moba(za):
    za3 = za.reshape(BATCH, SEQ, CUT_A)
    out = pl.pallas_call(
        _moba_kernel,
        grid=(BATCH, N_ATT_BLOCKS),
        in_specs=[pl.BlockSpec((None, SEQ, CUT_A), lambda b, i: (b, 0, 0))],
        out_specs=pl.BlockSpec((None, ATT_BLOCK, W_MIX), lambda b, i: (b, i, 0)),
        out_shape=jax.ShapeDtypeStruct((BATCH, SEQ, W_MIX), BF16),
        scratch_shapes=[
            pltpu.VMEM((N_HEADS * N_ATT_BLOCKS, W_MIX), BF16),
            pltpu.VMEM((N_HEADS * N_ATT_BLOCKS, W_MIX), BF16),
            pltpu.VMEM((N_ATT_BLOCKS, N_HEADS * HEAD_ROWS, ATT_BLOCK), BF16),
            pltpu.VMEM((N_ATT_BLOCKS, 1, N_HEADS * ATT_BLOCK), F32),
            pltpu.VMEM((ATT_BLOCK, N_HEADS * ATT_BLOCK), F32),
            pltpu.VMEM((ATT_BLOCK, N_HEADS * ATT_BLOCK), F32),
        ],
        compiler_params=pltpu.CompilerParams(
            dimension_semantics=("parallel", "arbitrary"), vmem_limit_bytes=VMEM_LIMIT),
        name="moba_attention",
    )(za3)
    return out.reshape(N_TOK, W_MIX)


def _dilated_bias_table():
    d = np.arange(ATT_BLOCK)[None, :] - np.arange(ATT_BLOCK)[:, None]
    tiles = []
    for k in range(N_ATT_BLOCKS):
        dist = d + k * ATT_BLOCK
        count = np.zeros(dist.shape, np.float64)
        for window, dil in DILATED_PATTERNS:
            count += (dist >= 0) & (dist <= window) & (dist % dil == 0)
        with np.errstate(divide="ignore"):
            tiles.append(np.log(count))
    return jnp.asarray(np.stack(tiles), F32)


def _dilated_kernel(z_ref, bias_ref, o_ref, vt_ref, sa_ref, sb_ref):
    i = pl.program_id(1)

    @pl.when(i == 0)
    def _prepare_batch():
        _store_transposed_v(z_ref, vt_ref)

    row0 = pl.multiple_of(i * ATT_BLOCK, ATT_BLOCK)
    q_stack = _stacked_queries(z_ref, row0)

    def scores(blk):
        r0 = pl.multiple_of(blk * ATT_BLOCK, ATT_BLOCK)
        s = _dot_nt(z_ref[pl.ds(r0, ATT_BLOCK), W_MIX:2 * W_MIX], q_stack)
        b = bias_ref[i - blk]
        return jnp.concatenate([s[:, _head_cols(h)] + b for h in range(N_HEADS)], axis=1)

    def bias_row(j):
        return jnp.full((1, N_HEADS * ATT_BLOCK), jnp.where(j < i, 0.0, MASK_BIAS), F32)

    _attend(i, scores(i), scores, bias_row, vt_ref, (sa_ref, sb_ref), o_ref)


def _dilated(zd, bias_table):
    zd3 = zd.reshape(BATCH, SEQ, 3 * W_MIX)
    out = pl.pallas_call(
        _dilated_kernel,
        grid=(BATCH, N_ATT_BLOCKS),
        in_specs=[
            pl.BlockSpec((None, SEQ, 3 * W_MIX), lambda b, i: (b, 0, 0)),
            pl.BlockSpec((N_ATT_BLOCKS, ATT_BLOCK, ATT_BLOCK), lambda b, i: (0, 0, 0)),
        ],
        out_specs=pl.BlockSpec((None, ATT_BLOCK, W_MIX), lambda b, i: (b, i, 0)),
        out_shape=jax.ShapeDtypeStruct((BATCH, SEQ, W_MIX), BF16),
        scratch_shapes=[pltpu.VMEM((N_ATT_BLOCKS, N_HEADS * HEAD_ROWS, ATT_BLOCK), BF16),
                        pltpu.VMEM((ATT_BLOCK, N_HEADS * ATT_BLOCK), F32),
                        pltpu.VMEM((ATT_BLOCK, N_HEADS * ATT_BLOCK), F32)],
        compiler_params=pltpu.CompilerParams(
            dimension_semantics=("parallel", "arbitrary"), vmem_limit_bytes=VMEM_LIMIT),
        name="dilated_attention",
    )(zd3, bias_table)
    return out.reshape(N_TOK, W_MIX)


def _gelu_tanh(x):
    return 0.5 * x * (1.0 + jnp.tanh(np.sqrt(2.0 / np.pi).astype(np.float32) * (x + 0.044715 * (x * x * x))))


def _sgu_conv_kernel(zb_ref, zc_ref, lng_ref, ws_ref, bs_ref, cw_ref, yb_ref, yc_ref):
    r = lax.broadcasted_iota(jnp.int32, (N_HEADS * SGU_CHUNK, SGU_CHUNK), 0)
    c = lax.broadcasted_iota(jnp.int32, (N_HEADS * SGU_CHUNK, SGU_CHUNK), 1)
    w_stack = jnp.where((r & (SGU_CHUNK - 1)) >= c, ws_ref[...], 0.0).astype(BF16)
    group = lax.broadcasted_iota(jnp.int32, (SGU_CHUNK, W_MIX), 1) >> 6
    ln_g = lng_ref[...]
    b_tile = bs_ref[...]

    def chunk(n, carry):
        r0 = pl.multiple_of(n * SGU_CHUNK, SGU_CHUNK)
        gz = _gelu_tanh(zb_ref[pl.ds(r0, SGU_CHUNK), :].astype(F32))
        u = gz[:, 0:W_MIX]
        v = gz[:, W_MIX:2 * W_MIX]
        vc = v - jnp.mean(v, axis=-1, keepdims=True)
        vn = vc * lax.rsqrt(jnp.mean(vc * vc, axis=-1, keepdims=True) + LN_EPS) * ln_g
        mixed = _dot(w_stack, vn.astype(BF16))
        sv = mixed[0:SGU_CHUNK]
        for gi in range(1, N_HEADS):
            sv = jnp.where(group == gi, mixed[gi * SGU_CHUNK:(gi + 1) * SGU_CHUNK], sv)
        yb_ref[pl.ds(r0, SGU_CHUNK), :] = (u * (sv + b_tile)).astype(yb_ref.dtype)
        return carry

    lax.fori_loop(0, SEQ // SGU_CHUNK, chunk, 0)

    zc = zc_ref[...].astype(F32)
    z = zc[:, W_MIX:2 * W_MIX] * zc[:, 2 * W_MIX:3 * W_MIX]
    t = lax.broadcasted_iota(jnp.int32, (SEQ, W_MIX), 0)
    z1 = jnp.where(t >= 1, pltpu.roll(z, 1, axis=0), 0.0)
    z2 = jnp.where(t >= 2, pltpu.roll(z, 2, axis=0), 0.0)
    y = z2 * cw_ref[0:1, :]
    y = y + z1 * cw_ref[1:2, :]
    y = y + z * cw_ref[2:3, :]
    yc_ref[...] = (zc[:, 0:W_MIX] * y).astype(yc_ref.dtype)


def _sgu_conv(zb, zc, ln_g, w_s, b_s, conv_w):
    w_stack = w_s.reshape(N_HEADS * SGU_CHUNK, SGU_CHUNK)
    b_tile = jnp.repeat(b_s.T, HEAD_DIM, axis=1)
    yb, yc = pl.pallas_call(
        _sgu_conv_kernel,
        grid=(BATCH,),
        in_specs=[
            pl.BlockSpec((None, SEQ, 2 * W_MIX), lambda b: (b, 0, 0)),
            pl.BlockSpec((None, SEQ, 3 * W_MIX), lambda b: (b, 0, 0)),
            pl.BlockSpec((1, W_MIX), lambda b: (0, 0)),
            pl.BlockSpec((N_HEADS * SGU_CHUNK, SGU_CHUNK), lambda b: (0, 0)),
            pl.BlockSpec((SGU_CHUNK, W_MIX), lambda b: (0, 0)),
            pl.BlockSpec((3, W_MIX), lambda b: (0, 0)),
        ],
        out_specs=[pl.BlockSpec((None, SEQ, W_MIX), lambda b: (b, 0, 0))] * 2,
        out_shape=[jax.ShapeDtypeStruct((BATCH, SEQ, W_MIX), BF16)] * 2,
        compiler_params=pltpu.CompilerParams(
            dimension_semantics=("parallel",), vmem_limit_bytes=VMEM_LIMIT),
        name="sgu_conv",
    )(zb.reshape(BATCH, SEQ, 2 * W_MIX), zc.reshape(BATCH, SEQ, 3 * W_MIX),
      ln_g.reshape(1, W_MIX), w_stack, b_tile, conv_w)
    return yb.reshape(N_TOK, W_MIX), yc.reshape(N_TOK, W_MIX)


def _outproj_residual(x_ref, ya_ref, yb_ref, yc_ref, yd_ref, w_ref):
    acc = _dot(ya_ref[...], w_ref[0:W_MIX, :])
    acc = acc + _dot(yb_ref[...], w_ref[W_MIX:2 * W_MIX, :])
    acc = acc + _dot(yc_ref[...], w_ref[2 * W_MIX:3 * W_MIX, :])
    acc = acc + _dot(yd_ref[...], w_ref[3 * W_MIX:4 * W_MIX, :])
    return x_ref[...] + acc


def _outproj_dense_kernel(x_ref, ya_ref, yb_ref, yc_ref, yd_ref, w_ref, g_ref, x1_ref, h_ref):
    x1 = _outproj_residual(x_ref, ya_ref, yb_ref, yc_ref, yd_ref, w_ref)
    x1_ref[...] = x1
    h_ref[...] = _rms(x1, g_ref[...]).astype(BF16)


_R_E1, _R_E2, _R_G1, _R_G2, _R_RANK1, _R_RANK2 = range(6)


def _outproj_router_kernel(x_ref, ya_ref, yb_ref, yc_ref, yd_ref, w_ref, g_ref, wr_hi_ref, wr_lo_ref,
                           x1_ref, h_ref, route_ref, counts_ref, running_ref):
    step = pl.program_id(0)

    @pl.when(step == 0)
    def _():
        running_ref[...] = jnp.zeros_like(running_ref)

    x1 = _outproj_residual(x_ref, ya_ref, yb_ref, yc_ref, yd_ref, w_ref)
    x1_ref[...] = x1
    h = _rms(x1, g_ref[...])
    h_ref[...] = h

    h_hi, h_lo = _split_bf16(h)
    logits = _dot(h_hi, wr_hi_ref[...]) + (_dot(h_lo, wr_hi_ref[...]) + _dot(h_hi, wr_lo_ref[...]))
    lane = lax.broadcasted_iota(jnp.int32, (ROW_TILE, LANES), 1)
    lane_f = lane.astype(F32)
    logits = jnp.where(lane < N_EXPERTS, logits, -jnp.inf)
    m1 = jnp.max(logits, axis=-1, keepdims=True)
    e1 = jnp.min(jnp.where(logits == m1, lane_f, float(LANES)), axis=-1, keepdims=True)
    rest = jnp.where(lane_f == e1, -jnp.inf, logits)
    m2 = jnp.max(rest, axis=-1, keepdims=True)
    e2 = jnp.min(jnp.where(rest == m2, lane_f, float(LANES)), axis=-1, keepdims=True)
    t = jnp.exp(m2 - m1)
    g1 = 1.0 / (1.0 + t)
    g2 = t / (1.0 + t)

    chosen = jnp.where((lane_f == e1) | (lane_f == e2), 1.0, 0.0)
    ri = lax.broadcasted_iota(jnp.int32, (ROW_TILE, ROW_TILE), 0)
    ci = lax.broadcasted_iota(jnp.int32, (ROW_TILE, ROW_TILE), 1)
    earlier = jnp.where(ci < ri, 1.0, 0.0).astype(BF16)
    before = _dot(earlier, chosen.astype(BF16)) + running_ref[...]
    rank1 = jnp.sum(jnp.where(lane_f == e1, before, 0.0), axis=-1, keepdims=True)
    rank2 = jnp.sum(jnp.where(lane_f == e2, before, 0.0), axis=-1, keepdims=True)
    running_ref[...] = running_ref[...] + jnp.sum(chosen, axis=0, keepdims=True)
    counts_ref[...] = running_ref[...]

    rec = jnp.zeros((ROW_TILE, LANES), F32)
    for pos, val in ((_R_E1, e1), (_R_E2, e2), (_R_G1, g1), (_R_G2, g2),
                     (_R_RANK1, rank1), (_R_RANK2, rank2)):
        rec = jnp.where(lane == pos, val, rec)
    route_ref[...] = rec


def _outproj(x, ys, w, g, router=None):
    row = lambda width: pl.BlockSpec((ROW_TILE, width), lambda i: (i, 0))
    const = lambda shape: pl.BlockSpec(shape, lambda i: (0, 0))
    in_specs = [row(D_MODEL)] + [row(W_MIX)] * 4 + [const((D_MODEL, D_MODEL)), const((1, D_MODEL))]
    if router is None:
        return pl.pallas_call(
            _outproj_dense_kernel,
            grid=(N_TOK // ROW_TILE,),
            in_specs=in_specs,
            out_specs=[row(D_MODEL), row(D_MODEL)],
            out_shape=[jax.ShapeDtypeStruct((N_TOK, D_MODEL), F32),
                       jax.ShapeDtypeStruct((N_TOK, D_MODEL), BF16)],
            compiler_params=pltpu.CompilerParams(
                dimension_semantics=("parallel",), vmem_limit_bytes=VMEM_LIMIT),
            name="outproj_norm",
        )(x, *ys, w, g)
    wr_hi, wr_lo = router
    return pl.pallas_call(
        _outproj_router_kernel,
        grid=(N_TOK // ROW_TILE,),
        in_specs=in_specs + [const((D_MODEL, LANES)), const((D_MODEL, LANES))],
        out_specs=[row(D_MODEL), row(D_MODEL), row(LANES), const((1, LANES))],
        out_shape=[jax.ShapeDtypeStruct((N_TOK, D_MODEL), F32),
                   jax.ShapeDtypeStruct((N_TOK, D_MODEL), F32),
                   jax.ShapeDtypeStruct((N_TOK, LANES), F32),
                   jax.ShapeDtypeStruct((1, LANES), F32)],
        scratch_shapes=[pltpu.VMEM((1, LANES), F32)],
        compiler_params=pltpu.CompilerParams(
            dimension_semantics=("arbitrary",), vmem_limit_bytes=VMEM_LIMIT),
        name="outproj_router",
    )(x, *ys, w, g, wr_hi, wr_lo)


def _silu_gate(gate, up):
    return (gate * (1.0 / (1.0 + jnp.exp(-gate))) * up).astype(BF16)


def _swiglu_rows(h, wg_ref, wu_ref, wd_ref):
    acc = None
    for c0 in range(0, D_FF, FF_CHUNK):
        act = _silu_gate(_dot(h, wg_ref[:, c0:c0 + FF_CHUNK]), _dot(h, wu_ref[:, c0:c0 + FF_CHUNK]))
        part = _dot(act, wd_ref[c0:c0 + FF_CHUNK, :])
        acc = part if acc is None else acc + part
    return acc


def _swiglu_rows_spreading(h, wg_ref, wu_ref, wd_ref, copies):
    copies = list(copies)
    n_gaps = 2 * (D_FF // MXU_TILE) + D_MODEL // MXU_TILE
    per_gap = -(-len(copies) // n_gaps)

    def start_some():
        for c in copies[:per_gap]:
            c.start()
        del copies[:per_gap]

    acts = []
    for c0 in range(0, D_FF, MXU_TILE):
        gate = _dot(h, wg_ref[:, c0:c0 + MXU_TILE])
        start_some()
        up = _dot(h, wu_ref[:, c0:c0 + MXU_TILE])
        start_some()
        acts.append(_silu_gate(gate, up))
    act = jnp.concatenate(acts, axis=1)
    outs = []
    for n0 in range(0, D_MODEL, MXU_TILE):
        outs.append(_dot(act, wd_ref[:, n0:n0 + MXU_TILE]))
        start_some()
    return jnp.concatenate(outs, axis=1)


def _dense_ffn_kernel(x_ref, h_ref, wg_ref, wu_ref, wd_ref, o_ref):
    o_ref[...] = x_ref[...] + _swiglu_rows(h_ref[...], wg_ref, wu_ref, wd_ref)


def _dense_ffn(x1, h, wg, wu, wd):
    row = lambda: pl.BlockSpec((ROW_TILE, D_MODEL), lambda i: (i, 0))
    once = pl.Buffered(1)
    return pl.pallas_call(
        _dense_ffn_kernel,
        grid=(N_TOK // ROW_TILE,),
        in_specs=[
            row(), row(),
            pl.BlockSpec((D_MODEL, D_FF), lambda i: (0, 0), pipeline_mode=once),
            pl.BlockSpec((D_MODEL, D_FF), lambda i: (0, 0), pipeline_mode=once),
            pl.BlockSpec((D_FF, D_MODEL), lambda i: (0, 0), pipeline_mode=once),
        ],
        out_specs=row(),
        out_shape=jax.ShapeDtypeStruct((N_TOK, D_MODEL), F32),
        compiler_params=pltpu.CompilerParams(
            dimension_semantics=("parallel",), vmem_limit_bytes=VMEM_LIMIT),
        name="dense_swiglu",
    )(x1, h, wg, wu, wd)


def _row_copy(src_ref, src_row, dst_ref, dst_row, sem):
    return pltpu.make_async_copy(src_ref.at[pl.ds(src_row, 1), :], dst_ref.at[pl.ds(dst_row, 1), :], sem)


def _invert_kernel(dest_ref, pad_lo_ref, pad_hi_ref, src_row_ref, dst_row_ref):
    for e in range(N_EXPERTS + 1):
        def mark_pad(row, carry):
            src_row_ref[row] = 0
            dst_row_ref[row] = N_TOK + (row & (2 * MOE_BLOCK - 1))
            return carry

        lax.fori_loop(pad_lo_ref[e], pad_hi_ref[e], mark_pad, 0)

    def place(token, carry):
        row1 = dest_ref[2 * token]
        row2 = dest_ref[2 * token + 1]
        src_row_ref[row1] = token
        src_row_ref[row2] = token
        dst_row_ref[row1] = token
        dst_row_ref[row2] = Y_PLANE + token
        return carry

    lax.fori_loop(0, N_TOK, place, 0, unroll=8)


def _invert(dest_flat, pad_lo, pad_hi):
    return pl.pallas_call(
        _invert_kernel,
        grid_spec=pltpu.PrefetchScalarGridSpec(
            num_scalar_prefetch=3,
            grid=(1,),
            in_specs=[],
            out_specs=[pl.BlockSpec(memory_space=pltpu.SMEM)] * 2,
        ),
        out_shape=[jax.ShapeDtypeStruct((MOE_CAP,), jnp.int32)] * 2,
        name="moe_invert",
    )(dest_flat, pad_lo, pad_hi)


def _expert_kernel(block_e_ref, src_row_ref, dst_row_ref, h_ref, wg_ref, wu_ref, wd_ref, y_ref,
                   x_buf, y_buf, gather_sem, scatter_sem):
    del block_e_ref
    i = pl.program_id(0)
    last = pl.num_programs(0) - 1
    cur = i & 1
    nxt = 1 - cur

    def gather(block, buf):
        return [_row_copy(h_ref, src_row_ref[block * MOE_BLOCK + r], x_buf.at[buf], r, gather_sem.at[buf])
                for r in range(MOE_BLOCK)]

    def scatter(block, buf):
        return [_row_copy(y_buf.at[buf], r, y_ref, dst_row_ref[block * MOE_BLOCK + r], scatter_sem.at[buf])
                for r in range(MOE_BLOCK)]

    prev = jnp.maximum(i - 1, 0)

    @pl.when(i == 0)
    def _():
        y_buf[...] = jnp.zeros_like(y_buf)
        for c in gather(0, 0):
            c.start()
        for b in range(2):
            dump = pltpu.make_async_copy(
                y_buf.at[b], y_ref.at[pl.ds(N_TOK + b * MOE_BLOCK, MOE_BLOCK), :], scatter_sem.at[b])
            dump.start()
            dump.wait()

    for c in gather(i, cur):
        c.wait()

    @pl.when(i >= 1)
    def _():
        for c in scatter(jnp.maximum(i - 2, 0), cur):
            c.wait()

    copies = gather(jnp.minimum(i + 1, last), nxt) + scatter(prev, nxt)
    y_buf[cur] = _swiglu_rows_spreading(x_buf[cur].astype(BF16), wg_ref, wu_ref, wd_ref, copies)

    @pl.when(i == last)
    def _():
        for c in scatter(last, cur):
            c.start()
        for c in gather(last, nxt):
            c.wait()
        for c in scatter(prev, nxt):
            c.wait()
        for c in scatter(last, cur):
            c.wait()


def _experts(block_e, src_row, dst_row, h, wg, wu, wd):
    weights = lambda shape: pl.BlockSpec((None,) + shape, lambda i, be, sr, dr: (be[i], 0, 0))
    return pl.pallas_call(
        _expert_kernel,
        grid_spec=pltpu.PrefetchScalarGridSpec(
            num_scalar_prefetch=3,
            grid=(N_MOE_BLOCKS,),
            in_specs=[pl.BlockSpec(memory_space=pl.ANY),
                      weights((D_MODEL, D_FF)), weights((D_MODEL, D_FF)), weights((D_FF, D_MODEL))],
            out_specs=pl.BlockSpec(memory_space=pl.ANY),
            scratch_shapes=[pltpu.VMEM((2, MOE_BLOCK, D_MODEL), F32),
                            pltpu.VMEM((2, MOE_BLOCK, D_MODEL), F32),
                            pltpu.SemaphoreType.DMA((2,)),
                            pltpu.SemaphoreType.DMA((2,))],
        ),
        out_shape=jax.ShapeDtypeStruct((Y_PLANE + N_TOK, D_MODEL), F32),
        compiler_params=pltpu.CompilerParams(
            dimension_semantics=("arbitrary",), vmem_limit_bytes=VMEM_LIMIT),
        name="moe_experts",
    )(block_e, src_row, dst_row, h, wg, wu, wd)


def _combine_kernel(x_ref, route_ref, g_ref, y1_ref, y2_ref, o_ref):
    route = route_ref[...]
    g1 = route[:, _R_G1:_R_G1 + 1]
    g2 = route[:, _R_G2:_R_G2 + 1]
    x2 = x_ref[...] + (y1_ref[...] * g1 + y2_ref[...] * g2)
    o_ref[...] = _rms(x2, g_ref[...])


def _combine_final_norm(x1, route, g_final, y):
    row = lambda width: pl.BlockSpec((ROW_TILE, width), lambda i: (i, 0))
    return pl.pallas_call(
        _combine_kernel,
        grid=(N_TOK // ROW_TILE,),
        in_specs=[row(D_MODEL), row(LANES), pl.BlockSpec((1, D_MODEL), lambda i: (0, 0)),
                  row(D_MODEL),
                  pl.BlockSpec((ROW_TILE, D_MODEL), lambda i: (Y_PLANE // ROW_TILE + i, 0))],
        out_specs=row(D_MODEL),
        out_shape=jax.ShapeDtypeStruct((N_TOK, D_MODEL), F32),
        compiler_params=pltpu.CompilerParams(
            dimension_semantics=("parallel",), vmem_limit_bytes=VMEM_LIMIT),
        name="moe_combine_norm",
    )(x1, route, g_final, y, y)


def _final_norm_kernel(x_ref, g_ref, o_ref):
    o_ref[...] = _rms(x_ref[...], g_ref[...])


def _final_norm(x, g):
    return pl.pallas_call(
        _final_norm_kernel,
        grid=(N_TOK // ROW_TILE,),
        in_specs=[pl.BlockSpec((ROW_TILE, D_MODEL), lambda i: (i, 0)),
                  pl.BlockSpec((1, D_MODEL), lambda i: (0, 0))],
        out_specs=pl.BlockSpec((ROW_TILE, D_MODEL), lambda i: (i, 0)),
        out_shape=jax.ShapeDtypeStruct((N_TOK, D_MODEL), F32),
        compiler_params=pltpu.CompilerParams(dimension_semantics=("parallel",)),
        name="final_norm",
    )(x, g)


def _expert_row_plan(route, counts):
    experts = route[:, _R_E1:_R_E2 + 1].astype(jnp.int32)
    ranks = route[:, _R_RANK1:_R_RANK2 + 1].astype(jnp.int32)
    counts = counts[0, :N_EXPERTS].astype(jnp.int32)
    padded = ((counts + MOE_BLOCK - 1) // MOE_BLOCK) * MOE_BLOCK
    ends = jnp.cumsum(padded)
    starts = ends - padded
    start_of = jnp.sum(jnp.where(experts[..., None] == jnp.arange(N_EXPERTS), starts, 0), axis=-1)
    dest = (start_of + ranks).reshape(-1)
    block_row = jnp.arange(N_MOE_BLOCKS, dtype=jnp.int32) * MOE_BLOCK
    block_e = jnp.minimum(jnp.sum(block_row[:, None] >= ends[None, :], axis=-1), N_EXPERTS - 1)
    pad_lo = jnp.concatenate([starts + counts, ends[-1:]])
    pad_hi = jnp.concatenate([ends, jnp.full((1,), MOE_CAP, jnp.int32)])
    return dest.astype(jnp.int32), block_e.astype(jnp.int32), pad_lo.astype(jnp.int32), pad_hi.astype(jnp.int32)


def kernel(x, g_mix, w_in, sgu_ln_g, sgu_w, sgu_b, conv_w, w_out, g_ffn, dense_w_gate, dense_w_up,
           dense_w_down, router_w, moe_w_gate, moe_w_up, moe_w_down, g_final):
    x = x.reshape(N_TOK, D_MODEL)
    bias_table = _dilated_bias_table()
    out = None
    for l in range(DEPTH):
        za, zb, zc, zd = _norm_inproj(x, g_mix[l].reshape(1, D_MODEL), w_in[l].astype(BF16))
        ya = _moba(za)
        yd = _dilated(zd, bias_table)
        yb, yc = _sgu_conv(zb, zc, sgu_ln_g[l], sgu_w[l], sgu_b[l], conv_w[l])
        ys = (ya, yb, yc, yd)
        g2 = g_ffn[l].reshape(1, D_MODEL)
        j = l // 2
        if l % 2 == 0:
            x1, h = _outproj(x, ys, w_out[l].astype(BF16), g2)
            x = _dense_ffn(x1, h, dense_w_gate[j].astype(BF16), dense_w_up[j].astype(BF16),
                           dense_w_down[j].astype(BF16))
            if l == DEPTH - 1:
                out = _final_norm(x, g_final.reshape(1, D_MODEL))
        else:
            wr = jnp.pad(router_w[j], ((0, 0), (0, LANES - N_EXPERTS)))
            x1, h, route, counts = _outproj(x, ys, w_out[l].astype(BF16), g2, router=_split_bf16(wr))
            dest, block_e, pad_lo, pad_hi = _expert_row_plan(route, counts)
            src_row, dst_row = _invert(dest, pad_lo, pad_hi)
            y = _experts(block_e, src_row, dst_row, h, moe_w_gate[j].astype(BF16), moe_w_up[j].astype(BF16),
                         moe_w_down[j].astype(BF16))
            if l == DEPTH - 1:
                out = _combine_final_norm(x1, route, g_final.reshape(1, D_MODEL), y)
            else:
                raise NotImplementedError("an expert layer that is not the last layer")
    return out.reshape(BATCH, SEQ, D_MODEL)
```

```python
import functools

import numpy as np
import jax
import jax.numpy as jnp
from jax import lax
from jax.experimental import pallas as pl
from jax.experimental.pallas import tpu as pltpu

D_MODEL = 1024
BATCH = 8
SEQ = 2048
DEPTH = 2
N_TOK = BATCH * SEQ
HEAD_DIM = 64
N_HEADS = 4
W_MIX = N_HEADS * HEAD_DIM
CUT_A = 3 * W_MIX
CUT_B = CUT_A + 2 * W_MIX
CUT_C = CUT_B + 3 * W_MIX
IN_COLS = CUT_C + 3 * W_MIX
ATT_BLOCK = 256
HEAD_ROWS = HEAD_DIM + 16
N_ATT_BLOCKS = SEQ // ATT_BLOCK
MOBA_TOPK = 3
SGU_CHUNK = 128
DILATED_PATTERNS = ((128, 1), (512, 4), (2048, 16))
D_FF = 2816
FF_CHUNK = D_FF // 2
N_EXPERTS = 8
MOE_BLOCK = 256
N_MOE_BLOCKS = (N_TOK * 2) // MOE_BLOCK + N_EXPERTS
MOE_CAP = N_MOE_BLOCKS * MOE_BLOCK
Y_PLANE = N_TOK + 2 * MOE_BLOCK
RING = 3
RMS_EPS = 1e-6
LN_EPS = 1e-5
ATTN_SCALE = HEAD_DIM ** -0.5
MASK_BIAS = -1e30

LANES = 128
MXU_TILE = 256
ROW_TILE = 512
VMEM_LIMIT = 56 * 1024 * 1024

F32 = jnp.float32
BF16 = jnp.bfloat16


def _dot(a, b):
    return jnp.dot(a, b, preferred_element_type=F32)


def _dot_nt(a, b):
    return lax.dot_general(a, b, (((1,), (1,)), ((), ())), preferred_element_type=F32)


def _rms(x, g):
    return x * lax.rsqrt(jnp.mean(x * x, axis=-1, keepdims=True) + RMS_EPS) * g


def _split_bf16(x):
    hi = x.astype(BF16)
    lo = (x - hi.astype(F32)).astype(BF16)
    return hi, lo


def _norm_inproj_kernel(x_ref, g_ref, w_ref, za_ref, zb_ref, zc_ref, zd_ref):
    h = _rms(x_ref[...], g_ref[...]).astype(BF16)
    za_ref[...] = _dot(h, w_ref[:, 0:CUT_A]).astype(BF16)
    zb_ref[...] = _dot(h, w_ref[:, CUT_A:CUT_B]).astype(BF16)
    zc_ref[...] = _dot(h, w_ref[:, CUT_B:CUT_C]).astype(BF16)
    zd_ref[...] = _dot(h, w_ref[:, CUT_C:IN_COLS]).astype(BF16)


def _norm_inproj(x, g, w):
    widths = (CUT_A, CUT_B - CUT_A, CUT_C - CUT_B, IN_COLS - CUT_C)
    return pl.pallas_call(
        _norm_inproj_kernel,
        grid=(N_TOK // ROW_TILE,),
        in_specs=[
            pl.BlockSpec((ROW_TILE, D_MODEL), lambda i: (i, 0)),
            pl.BlockSpec((1, D_MODEL), lambda i: (0, 0)),
            pl.BlockSpec((D_MODEL, IN_COLS), lambda i: (0, 0)),
        ],
        out_specs=[pl.BlockSpec((ROW_TILE, w_), lambda i: (i, 0)) for w_ in widths],
        out_shape=[jax.ShapeDtypeStruct((N_TOK, w_), BF16) for w_ in widths],
        compiler_params=pltpu.CompilerParams(
            dimension_semantics=("parallel",), vmem_limit_bytes=VMEM_LIMIT),
        name="norm_inproj",
    )(x, g, w)


def _head_lane_mask(h, width):
    lane = lax.broadcasted_iota(jnp.int32, (ATT_BLOCK, width), 1)
    return (lane >= h * HEAD_DIM) & (lane < (h + 1) * HEAD_DIM)


def _store_transposed_v(z_ref, vt_ref):
    row = lax.broadcasted_iota(jnp.int32, (HEAD_ROWS - HEAD_DIM, ATT_BLOCK), 0)
    ones_row = jnp.where(row == 0, 1.0, 0.0).astype(BF16)
    for j in range(N_ATT_BLOCKS):
        vt = z_ref[j * ATT_BLOCK:(j + 1) * ATT_BLOCK, 2 * W_MIX:3 * W_MIX].astype(F32).T
        for h in range(N_HEADS):
            vt_ref[j, h * HEAD_ROWS:h * HEAD_ROWS + HEAD_DIM, :] = (
                vt[h * HEAD_DIM:(h + 1) * HEAD_DIM].astype(BF16))
            vt_ref[j, h * HEAD_ROWS + HEAD_DIM:(h + 1) * HEAD_ROWS, :] = ones_row


def _stacked_queries(z_ref, row0):
    qs = z_ref[pl.ds(row0, ATT_BLOCK), 0:W_MIX].astype(F32) * ATTN_SCALE
    return jnp.concatenate(
        [jnp.where(_head_lane_mask(h, W_MIX), qs, 0.0) for h in range(N_HEADS)], axis=0).astype(BF16)


def _head_cols(h):
    return slice(h * ATT_BLOCK, (h + 1) * ATT_BLOCK)


def _values_times_probs(vt_ref, blk, p):
    return jnp.concatenate(
        [_dot(vt_ref[blk, h * HEAD_ROWS:(h + 1) * HEAD_ROWS, :], p[:, _head_cols(h)]) for h in range(N_HEADS)],
        axis=0)


def _per_head_rows(a):
    return jnp.concatenate(
        [jnp.broadcast_to(a[:, _head_cols(h)], (HEAD_ROWS, ATT_BLOCK)) for h in range(N_HEADS)], axis=0)


def _softmax_first(s, vt_ref, blk):
    m = jnp.max(s, axis=0, keepdims=True)
    return m, _values_times_probs(vt_ref, blk, jnp.exp(s - m).astype(BF16))


def _softmax_step(s, bias_row, vt_ref, blk, state):
    m, acc = state
    m_new = jnp.maximum(m, jnp.max(s, axis=0, keepdims=True) + bias_row)
    alpha = jnp.exp(m - m_new)
    p = jnp.exp(s - (m_new - bias_row))
    acc = _per_head_rows(alpha) * acc + _values_times_probs(vt_ref, blk, p.astype(BF16))
    return m_new, acc


def _attend(i, own_scores, scores, bias_row, vt_ref, s_refs, o_ref):
    s_a, s_b = s_refs
    last = jnp.maximum(i - 1, 0)
    s_a[...] = own_scores
    s_b[...] = scores(0)
    state = _softmax_first(s_a[...], vt_ref, i)
    s_a[...] = scores(jnp.minimum(1, last))
    state = _softmax_step(s_b[...], bias_row(0), vt_ref, 0, state)

    def two_blocks(t, state):
        j0 = 2 * t + 1
        j1 = j0 + 1
        s_b[...] = scores(jnp.minimum(j1, last))
        state = _softmax_step(s_a[...], bias_row(j0), vt_ref, j0, state)
        s_a[...] = scores(jnp.minimum(j1 + 1, last))
        return _softmax_step(s_b[...], bias_row(j1), vt_ref, j1, state)

    _, acc = lax.fori_loop(0, i // 2, two_blocks, state)
    heads = []
    for h in range(N_HEADS):
        r0 = h * HEAD_ROWS
        heads.append(acc[r0:r0 + HEAD_DIM] / acc[r0 + HEAD_DIM:r0 + HEAD_DIM + 1])
    o_ref[...] = jnp.concatenate(heads, axis=0).T.astype(o_ref.dtype)


def _moba_kernel(z_ref, o_ref, kmt_hi_ref, kmt_lo_ref, vt_ref, bias_ref, sa_ref, sb_ref):
    i = pl.program_id(1)

    @pl.when(i == 0)
    def _prepare_batch():
        means = []
        for j in range(N_ATT_BLOCKS):
            kj = z_ref[j * ATT_BLOCK:(j + 1) * ATT_BLOCK, W_MIX:2 * W_MIX].astype(F32)
            means.append(jnp.sum(kj, axis=0, keepdims=True) * (1.0 / ATT_BLOCK))
        mt = jnp.concatenate(means * N_HEADS, axis=0)
        r = lax.broadcasted_iota(jnp.int32, mt.shape, 0)
        c = lax.broadcasted_iota(jnp.int32, mt.shape, 1)
        hi, lo = _split_bf16(jnp.where((c >> 6) == (r >> 3), mt, 0.0))
        kmt_hi_ref[...] = hi
        kmt_lo_ref[...] = lo
        _store_transposed_v(z_ref, vt_ref)

    row0 = pl.multiple_of(i * ATT_BLOCK, ATT_BLOCK)
    q = z_ref[pl.ds(row0, ATT_BLOCK), 0:W_MIX]
    q_stack = _stacked_queries(z_ref, row0)

    def scores(blk):
        r0 = pl.multiple_of(blk * ATT_BLOCK, ATT_BLOCK)
        return _dot_nt(z_ref[pl.ds(r0, ATT_BLOCK), W_MIX:2 * W_MIX], q_stack)

    gates = _dot_nt(kmt_hi_ref[...], q) + _dot_nt(kmt_lo_ref[...], q)
    blk = lax.broadcasted_iota(jnp.int32, (N_ATT_BLOCKS, ATT_BLOCK), 0)
    past = blk < i
    biases = []
    for h in range(N_HEADS):
        g = jnp.where(past, gates[h * N_ATT_BLOCKS:(h + 1) * N_ATT_BLOCKS], -jnp.inf)
        rank = jnp.zeros(g.shape, F32)
        for d in range(1, N_ATT_BLOCKS):
            lower = pltpu.roll(g, d, axis=0)
            rank = rank + jnp.where((blk >= d) & (lower >= g), 1.0, 0.0)
            upper = pltpu.roll(g, N_ATT_BLOCKS - d, axis=0)
            rank = rank + jnp.where((blk + d < N_ATT_BLOCKS) & (upper > g), 1.0, 0.0)
        biases.append(jnp.where(past & (rank < MOBA_TOPK), 0.0, MASK_BIAS))
    bias = jnp.concatenate(biases, axis=1)
    for j in range(N_ATT_BLOCKS):
        bias_ref[j] = bias[j:j + 1]

    ki = lax.broadcasted_iota(jnp.int32, (ATT_BLOCK, N_HEADS * ATT_BLOCK), 0)
    qi = lax.broadcasted_iota(jnp.int32, (ATT_BLOCK, N_HEADS * ATT_BLOCK), 1) & (ATT_BLOCK - 1)
    own = jnp.where(ki <= qi, scores(i), -jnp.inf)
    _attend(i, own, scores, lambda j: bias_ref[j], vt_ref, (sa_ref, sb_ref), o_ref)


def _moba(za):
    za3 = za.reshape(BATCH, SEQ, CUT_A)
    out = pl.pallas_call(
        _moba_kernel,
        grid=(BATCH, N_ATT_BLOCKS),
        in_specs=[pl.BlockSpec((None, SEQ, CUT_A), lambda b, i: (b, 0, 0))],
        out_specs=pl.BlockSpec((None, ATT_BLOCK, W_MIX), lambda b, i: (b, i, 0)),
        out_shape=jax.ShapeDtypeStruct((BATCH, SEQ, W_MIX), BF16),
        scratch_shapes=[
            pltpu.VMEM((N_HEADS * N_ATT_BLOCKS, W_MIX), BF16),
            pltpu.VMEM((N_HEADS * N_ATT_BLOCKS, W_MIX), BF16),
            pltpu.VMEM((N_ATT_BLOCKS, N_HEADS * HEAD_ROWS, ATT_BLOCK), BF16),
            pltpu.VMEM((N_ATT_BLOCKS, 1, N_HEADS * ATT_BLOCK), F32),
            pltpu.VMEM((ATT_BLOCK, N_HEADS * ATT_BLOCK), F32),
            pltpu.VMEM((ATT_BLOCK, N_HEADS * ATT_BLOCK), F32),
        ],
        compiler_params=pltpu.CompilerParams(
            dimension_semantics=("parallel", "arbitrary"), vmem_limit_bytes=VMEM_LIMIT),
        name="moba_attention",
    )(za3)
    return out.reshape(N_TOK, W_MIX)


def _dilated_bias_table():
    d = np.arange(ATT_BLOCK)[None, :] - np.arange(ATT_BLOCK)[:, None]
    tiles = []
    for k in range(N_ATT_BLOCKS):
        dist = d + k * ATT_BLOCK
        count = np.zeros(dist.shape, np.float64)
        for window, dil in DILATED_PATTERNS:
            count += (dist >= 0) & (dist <= window) & (dist % dil == 0)
        with np.errstate(divide="ignore"):
            tiles.append(np.log(count))
    return jnp.asarray(np.stack(tiles), F32)


def _dilated_kernel(z_ref, bias_ref, o_ref, vt_ref, sa_ref, sb_ref):
    i = pl.program_id(1)

    @pl.when(i == 0)
    def _prepare_batch():
        _store_transposed_v(z_ref, vt_ref)

    row0 = pl.multiple_of(i * ATT_BLOCK, ATT_BLOCK)
    q_stack = _stacked_queries(z_ref, row0)

    def scores(blk):
        r0 = pl.multiple_of(blk * ATT_BLOCK, ATT_BLOCK)
        s = _dot_nt(z_ref[pl.ds(r0, ATT_BLOCK), W_MIX:2 * W_MIX], q_stack)
        b = bias_ref[i - blk]
        return jnp.concatenate([s[:, _head_cols(h)] + b for h in range(N_HEADS)], axis=1)

    def bias_row(j):
        return jnp.full((1, N_HEADS * ATT_BLOCK), jnp.where(j < i, 0.0, MASK_BIAS), F32)

    _attend(i, scores(i), scores, bias_row, vt_ref, (sa_ref, sb_ref), o_ref)


def _dilated(zd, bias_table):
    zd3 = zd.reshape(BATCH, SEQ, 3 * W_MIX)
    out = pl.pallas_call(
        _dilated_kernel,
        grid=(BATCH, N_ATT_BLOCKS),
        in_specs=[
            pl.BlockSpec((None, SEQ, 3 * W_MIX), lambda b, i: (b, 0, 0)),
            pl.BlockSpec((N_ATT_BLOCKS, ATT_BLOCK, ATT_BLOCK), lambda b, i: (0, 0, 0)),
        ],
        out_specs=pl.BlockSpec((None, ATT_BLOCK, W_MIX), lambda b, i: (b, i, 0)),
        out_shape=jax.ShapeDtypeStruct((BATCH, SEQ, W_MIX), BF16),
        scratch_shapes=[pltpu.VMEM((N_ATT_BLOCKS, N_HEADS * HEAD_ROWS, ATT_BLOCK), BF16),
                        pltpu.VMEM((ATT_BLOCK, N_HEADS * ATT_BLOCK), F32),
                        pltpu.VMEM((ATT_BLOCK, N_HEADS * ATT_BLOCK), F32)],
        compiler_params=pltpu.CompilerParams(
            dimension_semantics=("parallel", "arbitrary"), vmem_limit_bytes=VMEM_LIMIT),
        name="dilated_attention",
    )(zd3, bias_table)
    return out.reshape(N_TOK, W_MIX)


def _gelu_tanh(x):
    return 0.5 * x * (1.0 + jnp.tanh(np.sqrt(2.0 / np.pi).astype(np.float32) * (x + 0.044715 * (x * x * x))))


def _sgu_conv_kernel(zb_ref, zc_ref, lng_ref, ws_ref, bs_ref, cw_ref, yb_ref, yc_ref):
    r = lax.broadcasted_iota(jnp.int32, (N_HEADS * SGU_CHUNK, SGU_CHUNK), 0)
    c = lax.broadcasted_iota(jnp.int32, (N_HEADS * SGU_CHUNK, SGU_CHUNK), 1)
    w_stack = jnp.where((r & (SGU_CHUNK - 1)) >= c, ws_ref[...], 0.0).astype(BF16)
    group = lax.broadcasted_iota(jnp.int32, (SGU_CHUNK, W_MIX), 1) >> 6
    ln_g = lng_ref[...]
    b_tile = bs_ref[...]

    def chunk(n, carry):
        r0 = pl.multiple_of(n * SGU_CHUNK, SGU_CHUNK)
        gz = _gelu_tanh(zb_ref[pl.ds(r0, SGU_CHUNK), :].astype(F32))
        u = gz[:, 0:W_MIX]
        v = gz[:, W_MIX:2 * W_MIX]
        vc = v - jnp.mean(v, axis=-1, keepdims=True)
        vn = vc * lax.rsqrt(jnp.mean(vc * vc, axis=-1, keepdims=True) + LN_EPS) * ln_g
        mixed = _dot(w_stack, vn.astype(BF16))
        sv = mixed[0:SGU_CHUNK]
        for gi in range(1, N_HEADS):
            sv = jnp.where(group == gi, mixed[gi * SGU_CHUNK:(gi + 1) * SGU_CHUNK], sv)
        yb_ref[pl.ds(r0, SGU_CHUNK), :] = (u * (sv + b_tile)).astype(yb_ref.dtype)
        return carry

    lax.fori_loop(0, SEQ // SGU_CHUNK, chunk, 0, unroll=4)

    zc = zc_ref[...].astype(F32)
    z = zc[:, W_MIX:2 * W_MIX] * zc[:, 2 * W_MIX:3 * W_MIX]
    t = lax.broadcasted_iota(jnp.int32, (SEQ, W_MIX), 0)
    z1 = jnp.where(t >= 1, pltpu.roll(z, 1, axis=0), 0.0)
    z2 = jnp.where(t >= 2, pltpu.roll(z, 2, axis=0), 0.0)
    y = z2 * cw_ref[0:1, :]
    y = y + z1 * cw_ref[1:2, :]
    y = y + z * cw_ref[2:3, :]
    yc_ref[...] = (zc[:, 0:W_MIX] * y).astype(yc_ref.dtype)


def _sgu_conv(zb, zc, ln_g, w_s, b_s, conv_w):
    w_stack = w_s.reshape(N_HEADS * SGU_CHUNK, SGU_CHUNK)
    b_tile = jnp.repeat(b_s.T, HEAD_DIM, axis=1)
    yb, yc = pl.pallas_call(
        _sgu_conv_kernel,
        grid=(BATCH,),
        in_specs=[
            pl.BlockSpec((None, SEQ, 2 * W_MIX), lambda b: (b, 0, 0)),
            pl.BlockSpec((None, SEQ, 3 * W_MIX), lambda b: (b, 0, 0)),
            pl.BlockSpec((1, W_MIX), lambda b: (0, 0)),
            pl.BlockSpec((N_HEADS * SGU_CHUNK, SGU_CHUNK), lambda b: (0, 0)),
            pl.BlockSpec((SGU_CHUNK, W_MIX), lambda b: (0, 0)),
            pl.BlockSpec((3, W_MIX), lambda b: (0, 0)),
        ],
        out_specs=[pl.BlockSpec((None, SEQ, W_MIX), lambda b: (b, 0, 0))] * 2,
        out_shape=[jax.ShapeDtypeStruct((BATCH, SEQ, W_MIX), BF16)] * 2,
        compiler_params=pltpu.CompilerParams(
            dimension_semantics=("parallel",), vmem_limit_bytes=VMEM_LIMIT),
        name="sgu_conv",
    )(zb.reshape(BATCH, SEQ, 2 * W_MIX), zc.reshape(BATCH, SEQ, 3 * W_MIX),
      ln_g.reshape(1, W_MIX), w_stack, b_tile, conv_w)
    return yb.reshape(N_TOK, W_MIX), yc.reshape(N_TOK, W_MIX)


def _outproj_residual(x_ref, ya_ref, yb_ref, yc_ref, yd_ref, w_ref):
    acc = _dot(ya_ref[...], w_ref[0:W_MIX, :])
    acc = acc + _dot(yb_ref[...], w_ref[W_MIX:2 * W_MIX, :])
    acc = acc + _dot(yc_ref[...], w_ref[2 * W_MIX:3 * W_MIX, :])
    acc = acc + _dot(yd_ref[...], w_ref[3 * W_MIX:4 * W_MIX, :])
    return x_ref[...] + acc


def _outproj_dense_kernel(x_ref, ya_ref, yb_ref, yc_ref, yd_ref, w_ref, g_ref, x1_ref, h_ref):
    x1 = _outproj_residual(x_ref, ya_ref, yb_ref, yc_ref, yd_ref, w_ref)
    x1_ref[...] = x1
    h_ref[...] = _rms(x1, g_ref[...]).astype(BF16)


_R_E1, _R_E2, _R_G1, _R_G2, _R_RANK1, _R_RANK2 = range(6)


def _outproj_router_kernel(x_ref, ya_ref, yb_ref, yc_ref, yd_ref, w_ref, g_ref, wr_hi_ref, wr_lo_ref,
                           x1_ref, h_ref, route_ref, counts_ref, running_ref):
    step = pl.program_id(0)

    @pl.when(step == 0)
    def _():
        running_ref[...] = jnp.zeros_like(running_ref)

    x1 = _outproj_residual(x_ref, ya_ref, yb_ref, yc_ref, yd_ref, w_ref)
    x1_ref[...] = x1
    h = _rms(x1, g_ref[...])
    h_ref[...] = h

    h_hi, h_lo = _split_bf16(h)
    logits = _dot(h_hi, wr_hi_ref[...]) + (_dot(h_lo, wr_hi_ref[...]) + _dot(h_hi, wr_lo_ref[...]))
    lane = lax.broadcasted_iota(jnp.int32, (ROW_TILE, LANES), 1)
    lane_f = lane.astype(F32)
    logits = jnp.where(lane < N_EXPERTS, logits, -jnp.inf)
    m1 = jnp.max(logits, axis=-1, keepdims=True)
    e1 = jnp.min(jnp.where(logits == m1, lane_f, float(LANES)), axis=-1, keepdims=True)
    rest = jnp.where(lane_f == e1, -jnp.inf, logits)
    m2 = jnp.max(rest, axis=-1, keepdims=True)
    e2 = jnp.min(jnp.where(rest == m2, lane_f, float(LANES)), axis=-1, keepdims=True)
    t = jnp.exp(m2 - m1)
    g1 = 1.0 / (1.0 + t)
    g2 = t / (1.0 + t)

    chosen = jnp.where((lane_f == e1) | (lane_f == e2), 1.0, 0.0)
    ri = lax.broadcasted_iota(jnp.int32, (ROW_TILE, ROW_TILE), 0)
    ci = lax.broadcasted_iota(jnp.int32, (ROW_TILE, ROW_TILE), 1)
    earlier = jnp.where(ci < ri, 1.0, 0.0).astype(BF16)
    before = _dot(earlier, chosen.astype(BF16)) + running_ref[...]
    rank1 = jnp.sum(jnp.where(lane_f == e1, before, 0.0), axis=-1, keepdims=True)
    rank2 = jnp.sum(jnp.where(lane_f == e2, before, 0.0), axis=-1, keepdims=True)
    running_ref[...] = running_ref[...] + jnp.sum(chosen, axis=0, keepdims=True)
    counts_ref[...] = running_ref[...]

    rec = jnp.zeros((ROW_TILE, LANES), F32)
    for pos, val in ((_R_E1, e1), (_R_E2, e2), (_R_G1, g1), (_R_G2, g2),
                     (_R_RANK1, rank1), (_R_RANK2, rank2)):
        rec = jnp.where(lane == pos, val, rec)
    route_ref[...] = rec


def _outproj(x, ys, w, g, router=None):
    row = lambda width: pl.BlockSpec((ROW_TILE, width), lambda i: (i, 0))
    const = lambda shape: pl.BlockSpec(shape, lambda i: (0, 0))
    in_specs = [row(D_MODEL)] + [row(W_MIX)] * 4 + [const((D_MODEL, D_MODEL)), const((1, D_MODEL))]
    if router is None:
        return pl.pallas_call(
            _outproj_dense_kernel,
            grid=(N_TOK // ROW_TILE,),
            in_specs=in_specs,
            out_specs=[row(D_MODEL), row(D_MODEL)],
            out_shape=[jax.ShapeDtypeStruct((N_TOK, D_MODEL), F32),
                       jax.ShapeDtypeStruct((N_TOK, D_MODEL), BF16)],
            compiler_params=pltpu.CompilerParams(
                dimension_semantics=("parallel",), vmem_limit_bytes=VMEM_LIMIT),
            name="outproj_norm",
        )(x, *ys, w, g)
    wr_hi, wr_lo = router
    return pl.pallas_call(
        _outproj_router_kernel,
        grid=(N_TOK // ROW_TILE,),
        in_specs=in_specs + [const((D_MODEL, LANES)), const((D_MODEL, LANES))],
        out_specs=[row(D_MODEL), row(D_MODEL), row(LANES), const((1, LANES))],
        out_shape=[jax.ShapeDtypeStruct((N_TOK, D_MODEL), F32),
                   jax.ShapeDtypeStruct((N_TOK, D_MODEL), F32),
                   jax.ShapeDtypeStruct((N_TOK, LANES), F32),
                   jax.ShapeDtypeStruct((1, LANES), F32)],
        scratch_shapes=[pltpu.VMEM((1, LANES), F32)],
        compiler_params=pltpu.CompilerParams(
            dimension_semantics=("arbitrary",), vmem_limit_bytes=VMEM_LIMIT),
        name="outproj_router",
    )(x, *ys, w, g, wr_hi, wr_lo)


def _silu_gate(gate, up):
    return (gate * (1.0 / (1.0 + jnp.exp(-gate))) * up).astype(BF16)


def _swiglu_rows(h, wg_ref, wu_ref, wd_ref):
    acc = None
    for c0 in range(0, D_FF, FF_CHUNK):
        act = _silu_gate(_dot(h, wg_ref[:, c0:c0 + FF_CHUNK]), _dot(h, wu_ref[:, c0:c0 + FF_CHUNK]))
        part = _dot(act, wd_ref[c0:c0 + FF_CHUNK, :])
        acc = part if acc is None else acc + part
    return acc


def _swiglu_rows_spreading(h, wg_ref, wu_ref, wd_ref, copies):
    copies = list(copies)
    n_gaps = 2 * (D_FF // MXU_TILE) + D_MODEL // MXU_TILE
    per_gap = -(-len(copies) // n_gaps)

    def start_some():
        for c in copies[:per_gap]:
            c.start()
        del copies[:per_gap]

    acts = []
    for c0 in range(0, D_FF, MXU_TILE):
        gate = _dot(h, wg_ref[:, c0:c0 + MXU_TILE])
        start_some()
        up = _dot(h, wu_ref[:, c0:c0 + MXU_TILE])
        start_some()
        acts.append(_silu_gate(gate, up))
    act = jnp.concatenate(acts, axis=1)
    outs = []
    for n0 in range(0, D_MODEL, MXU_TILE):
        outs.append(_dot(act, wd_ref[:, n0:n0 + MXU_TILE]))
        start_some()
    return jnp.concatenate(outs, axis=1)


def _dense_ffn_kernel(x_ref, h_ref, wg_ref, wu_ref, wd_ref, o_ref):
    o_ref[...] = x_ref[...] + _swiglu_rows(h_ref[...], wg_ref, wu_ref, wd_ref)


def _dense_ffn(x1, h, wg, wu, wd):
    row = lambda: pl.BlockSpec((ROW_TILE, D_MODEL), lambda i: (i, 0))
    once = pl.Buffered(1)
    return pl.pallas_call(
        _dense_ffn_kernel,
        grid=(N_TOK // ROW_TILE,),
        in_specs=[
            row(), row(),
            pl.BlockSpec((D_MODEL, D_FF), lambda i: (0, 0), pipeline_mode=once),
            pl.BlockSpec((D_MODEL, D_FF), lambda i: (0, 0), pipeline_mode=once),
            pl.BlockSpec((D_FF, D_MODEL), lambda i: (0, 0), pipeline_mode=once),
        ],
        out_specs=row(),
        out_shape=jax.ShapeDtypeStruct((N_TOK, D_MODEL), F32),
        compiler_params=pltpu.CompilerParams(
            dimension_semantics=("parallel",), vmem_limit_bytes=VMEM_LIMIT),
        name="dense_swiglu",
    )(x1, h, wg, wu, wd)


def _row_copy(src_ref, src_row, dst_ref, dst_row, sem):
    return pltpu.make_async_copy(src_ref.at[pl.ds(src_row, 1), :], dst_ref.at[pl.ds(dst_row, 1), :], sem)


def _invert_kernel(dest_ref, pad_lo_ref, pad_hi_ref, src_row_ref, dst_row_ref):
    for e in range(N_EXPERTS + 1):
        def mark_pad(row, carry):
            src_row_ref[row] = 0
            dst_row_ref[row] = N_TOK + (row & (2 * MOE_BLOCK - 1))
            return carry

        lax.fori_loop(pad_lo_ref[e], pad_hi_ref[e], mark_pad, 0)

    def place(token, carry):
        row1 = dest_ref[2 * token]
        row2 = dest_ref[2 * token + 1]
        src_row_ref[row1] = token
        src_row_ref[row2] = token
        dst_row_ref[row1] = token
        dst_row_ref[row2] = Y_PLANE + token
        return carry

    lax.fori_loop(0, N_TOK, place, 0, unroll=8)


def _invert(dest_flat, pad_lo, pad_hi):
    return pl.pallas_call(
        _invert_kernel,
        grid_spec=pltpu.PrefetchScalarGridSpec(
            num_scalar_prefetch=3,
            grid=(1,),
            in_specs=[],
            out_specs=[pl.BlockSpec(memory_space=pltpu.SMEM)] * 2,
        ),
        out_shape=[jax.ShapeDtypeStruct((MOE_CAP,), jnp.int32)] * 2,
        name="moe_invert",
    )(dest_flat, pad_lo, pad_hi)


def _expert_kernel(block_e_ref, src_row_ref, dst_row_ref, h_ref, wg_ref, wu_ref, wd_ref, y_ref,
                   x_buf, y_buf, gather_sem, scatter_sem):
    del block_e_ref
    i = pl.program_id(0)
    last = N_MOE_BLOCKS - 1
    cur = lax.rem(i, RING)
    other = lax.rem(i + 2, RING)

    def gather(block, buf):
        return [_row_copy(h_ref, src_row_ref[block * MOE_BLOCK + r], x_buf.at[buf], r, gather_sem.at[buf])
                for r in range(MOE_BLOCK)]

    def scatter(block, buf, to_dump=None):
        copies = []
        for r in range(MOE_BLOCK):
            row = dst_row_ref[block * MOE_BLOCK + r]
            if to_dump is not None:
                row = jnp.where(to_dump, N_TOK + MOE_BLOCK + r, row)
            copies.append(_row_copy(y_buf.at[buf], r, y_ref, row, scatter_sem.at[buf]))
        return copies

    @pl.when(i == 0)
    def _():
        y_buf[...] = jnp.zeros_like(y_buf)
        for b in range(2):
            for c in gather(b, b):
                c.start()
            dump = pltpu.make_async_copy(
                y_buf.at[b], y_ref.at[pl.ds(N_TOK + b * MOE_BLOCK, MOE_BLOCK), :], scatter_sem.at[b])
            dump.start()
            dump.wait()

    for c in gather(i, cur):
        c.wait()

    @pl.when(i >= 2)
    def _():
        for c in scatter(jnp.maximum(i - 3, 0), cur):
            c.wait()

    copies = (gather(jnp.minimum(i + 2, last), other)
              + scatter(jnp.maximum(i - 1, 0), other, to_dump=(i == 0)))
    y_buf[cur] = _swiglu_rows_spreading(x_buf[cur].astype(BF16), wg_ref, wu_ref, wd_ref, copies)

    @pl.when(i == last)
    def _():
        for c in scatter(last - 2, (last - 2) % RING):
            c.wait()
        for c in scatter(last, last % RING):
            c.start()
        for extra in (last + 1, last + 2):
            for c in gather(last, extra % RING):
                c.wait()
        for block in (last - 1, last):
            for c in scatter(block, block % RING):
                c.wait()


def _experts(block_e, src_row, dst_row, h, wg, wu, wd):
    weights = lambda shape: pl.BlockSpec((None,) + shape, lambda i, be, sr, dr: (be[i], 0, 0))
    return pl.pallas_call(
        _expert_kernel,
        grid_spec=pltpu.PrefetchScalarGridSpec(
            num_scalar_prefetch=3,
            grid=(N_MOE_BLOCKS,),
            in_specs=[pl.BlockSpec(memory_space=pl.ANY),
                      weights((D_MODEL, D_FF)), weights((D_MODEL, D_FF)), weights((D_FF, D_MODEL))],
            out_specs=pl.BlockSpec(memory_space=pl.ANY),
            scratch_shapes=[pltpu.VMEM((RING, MOE_BLOCK, D_MODEL), F32),
                            pltpu.VMEM((RING, MOE_BLOCK, D_MODEL), F32),
                            pltpu.SemaphoreType.DMA((RING,)),
                            pltpu.SemaphoreType.DMA((RING,))],
        ),
        out_shape=jax.ShapeDtypeStruct((Y_PLANE + N_TOK, D_MODEL), F32),
        compiler_params=pltpu.CompilerParams(
            dimension_semantics=("arbitrary",), vmem_limit_bytes=VMEM_LIMIT),
        name="moe_experts",
    )(block_e, src_row, dst_row, h, wg, wu, wd)


def _combine_kernel(x_ref, route_ref, g_ref, y1_ref, y2_ref, o_ref):
    route = route_ref[...]
    g1 = route[:, _R_G1:_R_G1 + 1]
    g2 = route[:, _R_G2:_R_G2 + 1]
    x2 = x_ref[...] + (y1_ref[...] * g1 + y2_ref[...] * g2)
    o_ref[...] = _rms(x2, g_ref[...])


def _combine_final_norm(x1, route, g_final, y):
    row = lambda width: pl.BlockSpec((ROW_TILE, width), lambda i: (i, 0))
    return pl.pallas_call(
        _combine_kernel,
        grid=(N_TOK // ROW_TILE,),
        in_specs=[row(D_MODEL), row(LANES), pl.BlockSpec((1, D_MODEL), lambda i: (0, 0)),
                  row(D_MODEL),
                  pl.BlockSpec((ROW_TILE, D_MODEL), lambda i: (Y_PLANE // ROW_TILE + i, 0))],
        out_specs=row(D_MODEL),
        out_shape=jax.ShapeDtypeStruct((N_TOK, D_MODEL), F32),
        compiler_params=pltpu.CompilerParams(
            dimension_semantics=("parallel",), vmem_limit_bytes=VMEM_LIMIT),
        name="moe_combine_norm",
    )(x1, route, g_final, y, y)


def _final_norm_kernel(x_ref, g_ref, o_ref):
    o_ref[...] = _rms(x_ref[...], g_ref[...])


def _final_norm(x, g):
    return pl.pallas_call(
        _final_norm_kernel,
        grid=(N_TOK // ROW_TILE,),
        in_specs=[pl.BlockSpec((ROW_TILE, D_MODEL), lambda i: (i, 0)),
                  pl.BlockSpec((1, D_MODEL), lambda i: (0, 0))],
        out_specs=pl.BlockSpec((ROW_TILE, D_MODEL), lambda i: (i, 0)),
        out_shape=jax.ShapeDtypeStruct((N_TOK, D_MODEL), F32),
        compiler_params=pltpu.CompilerParams(dimension_semantics=("parallel",)),
        name="final_norm",
    )(x, g)


def _expert_row_plan(route, counts):
    experts = route[:, _R_E1:_R_E2 + 1].astype(jnp.int32)
    ranks = route[:, _R_RANK1:_R_RANK2 + 1].astype(jnp.int32)
    counts = counts[0, :N_EXPERTS].astype(jnp.int32)
    padded = ((counts + MOE_BLOCK - 1) // MOE_BLOCK) * MOE_BLOCK
    ends = jnp.cumsum(padded)
    starts = ends - padded
    start_of = jnp.sum(jnp.where(experts[..., None] == jnp.arange(N_EXPERTS), starts, 0), axis=-1)
    dest = (start_of + ranks).reshape(-1)
    block_row = jnp.arange(N_MOE_BLOCKS, dtype=jnp.int32) * MOE_BLOCK
    block_e = jnp.minimum(jnp.sum(block_row[:, None] >= ends[None, :], axis=-1), N_EXPERTS - 1)
    pad_lo = jnp.concatenate([starts + counts, ends[-1:]])
    pad_hi = jnp.concatenate([ends, jnp.full((1,), MOE_CAP, jnp.int32)])
    return dest.astype(jnp.int32), block_e.astype(jnp.int32), pad_lo.astype(jnp.int32), pad_hi.astype(jnp.int32)


def kernel(x, g_mix, w_in, sgu_ln_g, sgu_w, sgu_b, conv_w, w_out, g_ffn, dense_w_gate, dense_w_up,
           dense_w_down, router_w, moe_w_gate, moe_w_up, moe_w_down, g_final):
    x = x.reshape(N_TOK, D_MODEL)
    bias_table = _dilated_bias_table()
    out = None
    for l in range(DEPTH):
        za, zb, zc, zd = _norm_inproj(x, g_mix[l].reshape(1, D_MODEL), w_in[l].astype(BF16))
        ya = _moba(za)
        yd = _dilated(zd, bias_table)
        yb, yc = _sgu_conv(zb, zc, sgu_ln_g[l], sgu_w[l], sgu_b[l], conv_w[l])
        ys = (ya, yb, yc, yd)
        g2 = g_ffn[l].reshape(1, D_MODEL)
        j = l // 2
        if l % 2 == 0:
            x1, h = _outproj(x, ys, w_out[l].astype(BF16), g2)
            x = _dense_ffn(x1, h, dense_w_gate[j].astype(BF16), dense_w_up[j].astype(BF16),
                           dense_w_down[j].astype(BF16))
            if l == DEPTH - 1:
                out = _final_norm(x, g_final.reshape(1, D_MODEL))
        else:
            wr = jnp.pad(router_w[j], ((0, 0), (0, LANES - N_EXPERTS)))
            x1, h, route, counts = _outproj(x, ys, w_out[l].astype(BF16), g2, router=_split_bf16(wr))
            dest, block_e, pad_lo, pad_hi = _expert_row_plan(route, counts)
            src_row, dst_row = _invert(dest, pad_lo, pad_hi)
            y = _experts(block_e, src_row, dst_row, h, moe_w_gate[j].astype(BF16), moe_w_up[j].astype(BF16),
                         moe_w_down[j].astype(BF16))
            if l == DEPTH - 1:
                out = _combine_final_norm(x1, route, g_final.reshape(1, D_MODEL), y)
            else:
                raise NotImplementedError("an expert layer that is not the last layer")
    return out.reshape(BATCH, SEQ, D_MODEL)
```

```python
import functools

import numpy as np
import jax
import jax.numpy as jnp
from jax import lax
from jax.experimental import pallas as pl
from jax.experimental.pallas import tpu as pltpu

D_MODEL = 1024
BATCH = 8
SEQ = 2048
DEPTH = 2
N_TOK = BATCH * SEQ
HEAD_DIM = 64
N_HEADS = 4
W_MIX = N_HEADS * HEAD_DIM
CUT_A = 3 * W_MIX
CUT_B = CUT_A + 2 * W_MIX
CUT_C = CUT_B + 3 * W_MIX
IN_COLS = CUT_C + 3 * W_MIX
ATT_BLOCK = 256
HEAD_ROWS = HEAD_DIM + 16
N_ATT_BLOCKS = SEQ // ATT_BLOCK
MOBA_TOPK = 3
SGU_CHUNK = 128
DILATED_PATTERNS = ((128, 1), (512, 4), (2048, 16))
D_FF = 2816
FF_CHUNK = D_FF // 2
N_EXPERTS = 8
MOE_BLOCK = 256
N_MOE_BLOCKS = (N_TOK * 2) // MOE_BLOCK + N_EXPERTS
MOE_CAP = N_MOE_BLOCKS * MOE_BLOCK
Y_PLANE = N_TOK + 2 * MOE_BLOCK
RING = 3
RMS_EPS = 1e-6
LN_EPS = 1e-5
ATTN_SCALE = HEAD_DIM ** -0.5
MASK_BIAS = -1e30

LANES = 128
MXU_TILE = 256
ROW_TILE = 512
VMEM_LIMIT = 56 * 1024 * 1024

F32 = jnp.float32
BF16 = jnp.bfloat16


def _dot(a, b):
    return jnp.dot(a, b, preferred_element_type=F32)


def _dot_nt(a, b):
    return lax.dot_general(a, b, (((1,), (1,)), ((), ())), preferred_element_type=F32)


def _rms(x, g):
    return x * lax.rsqrt(jnp.mean(x * x, axis=-1, keepdims=True) + RMS_EPS) * g


def _split_bf16(x):
    hi = x.astype(BF16)
    lo = (x - hi.astype(F32)).astype(BF16)
    return hi, lo


def _norm_inproj_kernel(x_ref, g_ref, w_ref, za_ref, zb_ref, zc_ref, zd_ref):
    h = _rms(x_ref[...], g_ref[...]).astype(BF16)
    za_ref[...] = _dot(h, w_ref[:, 0:CUT_A]).astype(BF16)
    zb_ref[...] = _dot(h, w_ref[:, CUT_A:CUT_B]).astype(BF16)
    zc_ref[...] = _dot(h, w_ref[:, CUT_B:CUT_C]).astype(BF16)
    zd_ref[...] = _dot(h, w_ref[:, CUT_C:IN_COLS]).astype(BF16)


def _norm_inproj(x, g, w):
    widths = (CUT_A, CUT_B - CUT_A, CUT_C - CUT_B, IN_COLS - CUT_C)
    return pl.pallas_call(
        _norm_inproj_kernel,
        grid=(N_TOK // ROW_TILE,),
        in_specs=[
            pl.BlockSpec((ROW_TILE, D_MODEL), lambda i: (i, 0)),
            pl.BlockSpec((1, D_MODEL), lambda i: (0, 0)),
            pl.BlockSpec((D_MODEL, IN_COLS), lambda i: (0, 0)),
        ],
        out_specs=[pl.BlockSpec((ROW_TILE, w_), lambda i: (i, 0)) for w_ in widths],
        out_shape=[jax.ShapeDtypeStruct((N_TOK, w_), BF16) for w_ in widths],
        compiler_params=pltpu.CompilerParams(
            dimension_semantics=("parallel",), vmem_limit_bytes=VMEM_LIMIT),
        name="norm_inproj",
    )(x, g, w)


def _head_lane_mask(h, width):
    lane = lax.broadcasted_iota(jnp.int32, (ATT_BLOCK, width), 1)
    return (lane >= h * HEAD_DIM) & (lane < (h + 1) * HEAD_DIM)


def _store_transposed_v(z_ref, vt_ref):
    row = lax.broadcasted_iota(jnp.int32, (HEAD_ROWS - HEAD_DIM, ATT_BLOCK), 0)
    ones_row = jnp.where(row == 0, 1.0, 0.0).astype(BF16)
    for j in range(N_ATT_BLOCKS):
        vt = z_ref[j * ATT_BLOCK:(j + 1) * ATT_BLOCK, 2 * W_MIX:3 * W_MIX].astype(F32).T
        for h in range(N_HEADS):
            vt_ref[j, h * HEAD_ROWS:h * HEAD_ROWS + HEAD_DIM, :] = (
                vt[h * HEAD_DIM:(h + 1) * HEAD_DIM].astype(BF16))
            vt_ref[j, h * HEAD_ROWS + HEAD_DIM:(h + 1) * HEAD_ROWS, :] = ones_row


def _stacked_queries(z_ref, row0):
    qs = z_ref[pl.ds(row0, ATT_BLOCK), 0:W_MIX].astype(F32) * ATTN_SCALE
    return jnp.concatenate(
        [jnp.where(_head_lane_mask(h, W_MIX), qs, 0.0) for h in range(N_HEADS)], axis=0).astype(BF16)


def _head_cols(h):
    return slice(h * ATT_BLOCK, (h + 1) * ATT_BLOCK)


def _values_times_probs(vt_ref, blk, p):
    return jnp.concatenate(
        [_dot(vt_ref[blk, h * HEAD_ROWS:(h + 1) * HEAD_ROWS, :], p[:, _head_cols(h)]) for h in range(N_HEADS)],
        axis=0)


def _per_head_rows(a):
    return jnp.concatenate(
        [jnp.broadcast_to(a[:, _head_cols(h)], (HEAD_ROWS, ATT_BLOCK)) for h in range(N_HEADS)], axis=0)


def _softmax_first(s, vt_ref, blk):
    m = jnp.max(s, axis=0, keepdims=True)
    return m, _values_times_probs(vt_ref, blk, jnp.exp(s - m).astype(BF16))


def _softmax_step(s, bias_row, vt_ref, blk, state):
    m, acc = state
    m_new = jnp.maximum(m, jnp.max(s, axis=0, keepdims=True) + bias_row)
    alpha = jnp.exp(m - m_new)
    p = jnp.exp(s - (m_new - bias_row))
    acc = _per_head_rows(alpha) * acc + _values_times_probs(vt_ref, blk, p.astype(BF16))
    return m_new, acc


def _attend(i, own_scores, scores, bias_row, vt_ref, s_refs, o_ref):
    s_a, s_b = s_refs
    last = jnp.maximum(i - 1, 0)
    s_a[...] = own_scores
    s_b[...] = scores(0)
    state = _softmax_first(s_a[...], vt_ref, i)
    s_a[...] = scores(jnp.minimum(1, last))
    state = _softmax_step(s_b[...], bias_row(0), vt_ref, 0, state)

    def two_blocks(t, state):
        j0 = 2 * t + 1
        j1 = j0 + 1
        s_b[...] = scores(jnp.minimum(j1, last))
        state = _softmax_step(s_a[...], bias_row(j0), vt_ref, j0, state)
        s_a[...] = scores(jnp.minimum(j1 + 1, last))
        return _softmax_step(s_b[...], bias_row(j1), vt_ref, j1, state)

    _, acc = lax.fori_loop(0, i // 2, two_blocks, state)
    heads = []
    for h in range(N_HEADS):
        r0 = h * HEAD_ROWS
        heads.append(acc[r0:r0 + HEAD_DIM] / acc[r0 + HEAD_DIM:r0 + HEAD_DIM + 1])
    o_ref[...] = jnp.concatenate(heads, axis=0).T.astype(o_ref.dtype)


def _moba_kernel(z_ref, o_ref, kmt_hi_ref, kmt_lo_ref, vt_ref, bias_ref, sa_ref, sb_ref):
    i = pl.program_id(1)

    @pl.when(i == 0)
    def _prepare_batch():
        means = []
        for j in range(N_ATT_BLOCKS):
            kj = z_ref[j * ATT_BLOCK:(j + 1) * ATT_BLOCK, W_MIX:2 * W_MIX].astype(F32)
            means.append(jnp.sum(kj, axis=0, keepdims=True) * (1.0 / ATT_BLOCK))
        mt = jnp.concatenate(means * N_HEADS, axis=0)
        r = lax.broadcasted_iota(jnp.int32, mt.shape, 0)
        c = lax.broadcasted_iota(jnp.int32, mt.shape, 1)
        hi, lo = _split_bf16(jnp.where((c >> 6) == (r >> 3), mt, 0.0))
        kmt_hi_ref[...] = hi
        kmt_lo_ref[...] = lo
        _store_transposed_v(z_ref, vt_ref)

    row0 = pl.multiple_of(i * ATT_BLOCK, ATT_BLOCK)
    q = z_ref[pl.ds(row0, ATT_BLOCK), 0:W_MIX]
    q_stack = _stacked_queries(z_ref, row0)

    def scores(blk):
        r0 = pl.multiple_of(blk * ATT_BLOCK, ATT_BLOCK)
        return _dot_nt(z_ref[pl.ds(r0, ATT_BLOCK), W_MIX:2 * W_MIX], q_stack)

    gates = _dot_nt(kmt_hi_ref[...], q) + _dot_nt(kmt_lo_ref[...], q)
    blk = lax.broadcasted_iota(jnp.int32, (N_ATT_BLOCKS, ATT_BLOCK), 0)
    past = blk < i
    biases = []
    for h in range(N_HEADS):
        g = jnp.where(past, gates[h * N_ATT_BLOCKS:(h + 1) * N_ATT_BLOCKS], -jnp.inf)
        rank = jnp.zeros(g.shape, F32)
        for d in range(1, N_ATT_BLOCKS):
            lower = pltpu.roll(g, d, axis=0)
            rank = rank + jnp.where((blk >= d) & (lower >= g), 1.0, 0.0)
            upper = pltpu.roll(g, N_ATT_BLOCKS - d, axis=0)
            rank = rank + jnp.where((blk + d < N_ATT_BLOCKS) & (upper > g), 1.0, 0.0)
        biases.append(jnp.where(past & (rank < MOBA_TOPK), 0.0, MASK_BIAS))
    bias = jnp.concatenate(biases, axis=1)
    for j in range(N_ATT_BLOCKS):
        bias_ref[j] = bias[j:j + 1]

    ki = lax.broadcasted_iota(jnp.int32, (ATT_BLOCK, N_HEADS * ATT_BLOCK), 0)
    qi = lax.broadcasted_iota(jnp.int32, (ATT_BLOCK, N_HEADS * ATT_BLOCK), 1) & (ATT_BLOCK - 1)
    own = jnp.where(ki <= qi, scores(i), -jnp.inf)
    _attend(i, own, scores, lambda j: bias_ref[j], vt_ref, (sa_ref, sb_ref), o_ref)


def _moba(za):
    za3 = za.reshape(BATCH, SEQ, CUT_A)
    out = pl.pallas_call(
        _moba_kernel,
        grid=(BATCH, N_ATT_BLOCKS),
        in_specs=[pl.BlockSpec((None, SEQ, CUT_A), lambda b, i: (b, 0, 0))],
        out_specs=pl.BlockSpec((None, ATT_BLOCK, W_MIX), lambda b, i: (b, i, 0)),
        out_shape=jax.ShapeDtypeStruct((BATCH, SEQ, W_MIX), BF16),
        scratch_shapes=[
            pltpu.VMEM((N_HEADS * N_ATT_BLOCKS, W_MIX), BF16),
            pltpu.VMEM((N_HEADS * N_ATT_BLOCKS, W_MIX), BF16),
            pltpu.VMEM((N_ATT_BLOCKS, N_HEADS * HEAD_ROWS, ATT_BLOCK), BF16),
            pltpu.VMEM((N_ATT_BLOCKS, 1, N_HEADS * ATT_BLOCK), F32),
            pltpu.VMEM((ATT_BLOCK, N_HEADS * ATT_BLOCK), F32),
            pltpu.VMEM((ATT_BLOCK, N_HEADS * ATT_BLOCK), F32),
        ],
        compiler_params=pltpu.CompilerParams(
            dimension_semantics=("parallel", "arbitrary"), vmem_limit_bytes=VMEM_LIMIT),
        name="moba_attention",
    )(za3)
    return out.reshape(N_TOK, W_MIX)


def _dilated_bias_table():
    d = np.arange(ATT_BLOCK)[None, :] - np.arange(ATT_BLOCK)[:, None]
    tiles = []
    for k in range(N_ATT_BLOCKS):
        dist = d + k * ATT_BLOCK
        count = np.zeros(dist.shape, np.float64)
        for window, dil in DILATED_PATTERNS:
            count += (dist >= 0) & (dist <= window) & (dist % dil == 0)
        with np.errstate(divide="ignore"):
            tiles.append(np.log(count))
    return jnp.asarray(np.stack(tiles), F32)


def _dilated_kernel(z_ref, bias_ref, o_ref, vt_ref, sa_ref, sb_ref):
    i = pl.program_id(1)

    @pl.when(i == 0)
    def _prepare_batch():
        _store_transposed_v(z_ref, vt_ref)

    row0 = pl.multiple_of(i * ATT_BLOCK, ATT_BLOCK)
    q_stack = _stacked_queries(z_ref, row0)

    def scores(blk):
        r0 = pl.multiple_of(blk * ATT_BLOCK, ATT_BLOCK)
        s = _dot_nt(z_ref[pl.ds(r0, ATT_BLOCK), W_MIX:2 * W_MIX], q_stack)
        b = bias_ref[i - blk]
        return jnp.concatenate([s[:, _head_cols(h)] + b for h in range(N_HEADS)], axis=1)

    def bias_row(j):
        return jnp.full((1, N_HEADS * ATT_BLOCK), jnp.where(j < i, 0.0, MASK_BIAS), F32)

    _attend(i, scores(i), scores, bias_row, vt_ref, (sa_ref, sb_ref), o_ref)


def _dilated(zd, bias_table):
    zd3 = zd.reshape(BATCH, SEQ, 3 * W_MIX)
    out = pl.pallas_call(
        _dilated_kernel,
        grid=(BATCH, N_ATT_BLOCKS),
        in_specs=[
            pl.BlockSpec((None, SEQ, 3 * W_MIX), lambda b, i: (b, 0, 0)),
            pl.BlockSpec((N_ATT_BLOCKS, ATT_BLOCK, ATT_BLOCK), lambda b, i: (0, 0, 0)),
        ],
        out_specs=pl.BlockSpec((None, ATT_BLOCK, W_MIX), lambda b, i: (b, i, 0)),
        out_shape=jax.ShapeDtypeStruct((BATCH, SEQ, W_MIX), BF16),
        scratch_shapes=[pltpu.VMEM((N_ATT_BLOCKS, N_HEADS * HEAD_ROWS, ATT_BLOCK), BF16),
                        pltpu.VMEM((ATT_BLOCK, N_HEADS * ATT_BLOCK), F32),
                        pltpu.VMEM((ATT_BLOCK, N_HEADS * ATT_BLOCK), F32)],
        compiler_params=pltpu.CompilerParams(
            dimension_semantics=("parallel", "arbitrary"), vmem_limit_bytes=VMEM_LIMIT),
        name="dilated_attention",
    )(zd3, bias_table)
    return out.reshape(N_TOK, W_MIX)


def _gelu_tanh(x):
    return 0.5 * x * (1.0 + jnp.tanh(np.sqrt(2.0 / np.pi).astype(np.float32) * (x + 0.044715 * (x * x * x))))


def _sgu_conv_kernel(zb_ref, zc_ref, lng_ref, ws_ref, bs_ref, cw_ref, yb_ref, yc_ref):
    r = lax.broadcasted_iota(jnp.int32, (N_HEADS * SGU_CHUNK, SGU_CHUNK), 0)
    c = lax.broadcasted_iota(jnp.int32, (N_HEADS * SGU_CHUNK, SGU_CHUNK), 1)
    w_stack = jnp.where((r & (SGU_CHUNK - 1)) >= c, ws_ref[...], 0.0).astype(BF16)
    group = lax.broadcasted_iota(jnp.int32, (SGU_CHUNK, W_MIX), 1) >> 6
    ln_g = lng_ref[...]
    b_tile = bs_ref[...]

    def chunk(n, carry):
        r0 = pl.multiple_of(n * SGU_CHUNK, SGU_CHUNK)
        gz = _gelu_tanh(zb_ref[pl.ds(r0, SGU_CHUNK), :].astype(F32))
        u = gz[:, 0:W_MIX]
        v = gz[:, W_MIX:2 * W_MIX]
        vc = v - jnp.mean(v, axis=-1, keepdims=True)
        vn = vc * lax.rsqrt(jnp.mean(vc * vc, axis=-1, keepdims=True) + LN_EPS) * ln_g
        mixed = _dot(w_stack, vn.astype(BF16))
        sv = mixed[0:SGU_CHUNK]
        for gi in range(1, N_HEADS):
            sv = jnp.where(group == gi, mixed[gi * SGU_CHUNK:(gi + 1) * SGU_CHUNK], sv)
        yb_ref[pl.ds(r0, SGU_CHUNK), :] = (u * (sv + b_tile)).astype(yb_ref.dtype)
        return carry

    lax.fori_loop(0, SEQ // SGU_CHUNK, chunk, 0, unroll=4)

    zc = zc_ref[...].astype(F32)
    z = zc[:, W_MIX:2 * W_MIX] * zc[:, 2 * W_MIX:3 * W_MIX]
    t = lax.broadcasted_iota(jnp.int32, (SEQ, W_MIX), 0)
    z1 = jnp.where(t >= 1, pltpu.roll(z, 1, axis=0), 0.0)
    z2 = jnp.where(t >= 2, pltpu.roll(z, 2, axis=0), 0.0)
    y = z2 * cw_ref[0:1, :]
    y = y + z1 * cw_ref[1:2, :]
    y = y + z * cw_ref[2:3, :]
    yc_ref[...] = (zc[:, 0:W_MIX] * y).astype(yc_ref.dtype)


def _sgu_conv(zb, zc, ln_g, w_s, b_s, conv_w):
    w_stack = w_s.reshape(N_HEADS * SGU_CHUNK, SGU_CHUNK)
    b_tile = jnp.repeat(b_s.T, HEAD_DIM, axis=1)
    yb, yc = pl.pallas_call(
        _sgu_conv_kernel,
        grid=(BATCH,),
        in_specs=[
            pl.BlockSpec((None, SEQ, 2 * W_MIX), lambda b: (b, 0, 0)),
            pl.BlockSpec((None, SEQ, 3 * W_MIX), lambda b: (b, 0, 0)),
            pl.BlockSpec((1, W_MIX), lambda b: (0, 0)),
            pl.BlockSpec((N_HEADS * SGU_CHUNK, SGU_CHUNK), lambda b: (0, 0)),
            pl.BlockSpec((SGU_CHUNK, W_MIX), lambda b: (0, 0)),
            pl.BlockSpec((3, W_MIX), lambda b: (0, 0)),
        ],
        out_specs=[pl.BlockSpec((None, SEQ, W_MIX), lambda b: (b, 0, 0))] * 2,
        out_shape=[jax.ShapeDtypeStruct((BATCH, SEQ, W_MIX), BF16)] * 2,
        compiler_params=pltpu.CompilerParams(
            dimension_semantics=("parallel",), vmem_limit_bytes=VMEM_LIMIT),
        name="sgu_conv",
    )(zb.reshape(BATCH, SEQ, 2 * W_MIX), zc.reshape(BATCH, SEQ, 3 * W_MIX),
      ln_g.reshape(1, W_MIX), w_stack, b_tile, conv_w)
    return yb.reshape(N_TOK, W_MIX), yc.reshape(N_TOK, W_MIX)


def _outproj_residual(x_ref, ya_ref, yb_ref, yc_ref, yd_ref, w_ref):
    acc = _dot(ya_ref[...], w_ref[0:W_MIX, :])
    acc = acc + _dot(yb_ref[...], w_ref[W_MIX:2 * W_MIX, :])
    acc = acc + _dot(yc_ref[...], w_ref[2 * W_MIX:3 * W_MIX, :])
    acc = acc + _dot(yd_ref[...], w_ref[3 * W_MIX:4 * W_MIX, :])
    return x_ref[...] + acc


def _outproj_dense_kernel(x_ref, ya_ref, yb_ref, yc_ref, yd_ref, w_ref, g_ref, x1_ref, h_ref):
    x1 = _outproj_residual(x_ref, ya_ref, yb_ref, yc_ref, yd_ref, w_ref)
    x1_ref[...] = x1
    h_ref[...] = _rms(x1, g_ref[...]).astype(BF16)


_R_E1, _R_E2, _R_G1, _R_G2, _R_RANK1, _R_RANK2 = range(6)


def _outproj_router_kernel(x_ref, ya_ref, yb_ref, yc_ref, yd_ref, w_ref, g_ref, wra_ref, wrb_ref,
                           x1_ref, h_ref, route_ref, counts_ref, running_ref):
    step = pl.program_id(0)

    @pl.when(step == 0)
    def _():
        running_ref[...] = jnp.zeros_like(running_ref)

    x1 = _outproj_residual(x_ref, ya_ref, yb_ref, yc_ref, yd_ref, w_ref)
    x1_ref[...] = x1
    h = _rms(x1, g_ref[...])
    h_ref[...] = h

    h_hi, h_lo = _split_bf16(h)
    part = _dot_nt(wra_ref[...], h_hi)
    logits = part[0:N_EXPERTS] + (part[N_EXPERTS:2 * N_EXPERTS] + _dot_nt(wrb_ref[...], h_lo)[0:N_EXPERTS])
    expert = lax.broadcasted_iota(jnp.int32, (N_EXPERTS, ROW_TILE), 0).astype(F32)
    m1 = jnp.max(logits, axis=0, keepdims=True)
    e1 = jnp.min(jnp.where(logits == m1, expert, float(N_EXPERTS)), axis=0, keepdims=True)
    rest = jnp.where(expert == e1, -jnp.inf, logits)
    m2 = jnp.max(rest, axis=0, keepdims=True)
    e2 = jnp.min(jnp.where(rest == m2, expert, float(N_EXPERTS)), axis=0, keepdims=True)
    t = jnp.exp(m2 - m1)
    g1 = 1.0 / (1.0 + t)
    g2 = t / (1.0 + t)

    chosen = jnp.where((expert == e1) | (expert == e2), 1.0, 0.0)
    ri = lax.broadcasted_iota(jnp.int32, (ROW_TILE, ROW_TILE), 0)
    ci = lax.broadcasted_iota(jnp.int32, (ROW_TILE, ROW_TILE), 1)
    earlier = jnp.where(ri < ci, 1.0, 0.0).astype(BF16)
    chosen16 = jnp.concatenate([chosen, jnp.zeros_like(chosen)], axis=0).astype(BF16)
    before = _dot(chosen16, earlier)[0:N_EXPERTS] + running_ref[:, 0:1]
    rank1 = jnp.sum(jnp.where(expert == e1, before, 0.0), axis=0, keepdims=True)
    rank2 = jnp.sum(jnp.where(expert == e2, before, 0.0), axis=0, keepdims=True)
    running_ref[...] = running_ref[...] + jnp.sum(chosen, axis=1, keepdims=True)
    counts_ref[...] = running_ref[...]

    rec = jnp.zeros((N_EXPERTS, ROW_TILE), F32)
    for pos, val in ((_R_E1, e1), (_R_E2, e2), (_R_G1, g1), (_R_G2, g2),
                     (_R_RANK1, rank1), (_R_RANK2, rank2)):
        rec = jnp.where(expert == float(pos), val, rec)
    route_ref[...] = rec


def _outproj(x, ys, w, g, router=None):
    row = lambda width: pl.BlockSpec((ROW_TILE, width), lambda i: (i, 0))
    const = lambda shape: pl.BlockSpec(shape, lambda i: (0, 0))
    in_specs = [row(D_MODEL)] + [row(W_MIX)] * 4 + [const((D_MODEL, D_MODEL)), const((1, D_MODEL))]
    if router is None:
        return pl.pallas_call(
            _outproj_dense_kernel,
            grid=(N_TOK // ROW_TILE,),
            in_specs=in_specs,
            out_specs=[row(D_MODEL), row(D_MODEL)],
            out_shape=[jax.ShapeDtypeStruct((N_TOK, D_MODEL), F32),
                       jax.ShapeDtypeStruct((N_TOK, D_MODEL), BF16)],
            compiler_params=pltpu.CompilerParams(
                dimension_semantics=("parallel",), vmem_limit_bytes=VMEM_LIMIT),
            name="outproj_norm",
        )(x, *ys, w, g)
    wra, wrb = router
    return pl.pallas_call(
        _outproj_router_kernel,
        grid=(N_TOK // ROW_TILE,),
        in_specs=in_specs + [const((2 * N_EXPERTS, D_MODEL)), const((2 * N_EXPERTS, D_MODEL))],
        out_specs=[row(D_MODEL), row(D_MODEL),
                   pl.BlockSpec((N_EXPERTS, ROW_TILE), lambda i: (0, i)), const((N_EXPERTS, LANES))],
        out_shape=[jax.ShapeDtypeStruct((N_TOK, D_MODEL), F32),
                   jax.ShapeDtypeStruct((N_TOK, D_MODEL), F32),
                   jax.ShapeDtypeStruct((N_EXPERTS, N_TOK), F32),
                   jax.ShapeDtypeStruct((N_EXPERTS, LANES), F32)],
        scratch_shapes=[pltpu.VMEM((N_EXPERTS, LANES), F32)],
        compiler_params=pltpu.CompilerParams(
            dimension_semantics=("arbitrary",), vmem_limit_bytes=VMEM_LIMIT),
        name="outproj_router",
    )(x, *ys, w, g, wra, wrb)


def _silu_gate(gate, up):
    return (gate * (1.0 / (1.0 + jnp.exp(-gate))) * up).astype(BF16)


def _swiglu_rows(h, wg_ref, wu_ref, wd_ref):
    acc = None
    for c0 in range(0, D_FF, FF_CHUNK):
        act = _silu_gate(_dot(h, wg_ref[:, c0:c0 + FF_CHUNK]), _dot(h, wu_ref[:, c0:c0 + FF_CHUNK]))
        part = _dot(act, wd_ref[c0:c0 + FF_CHUNK, :])
        acc = part if acc is None else acc + part
    return acc


def _swiglu_rows_spreading(h, wg_ref, wu_ref, wd_ref, copies):
    copies = list(copies)
    n_gaps = 2 * (D_FF // MXU_TILE) + D_MODEL // MXU_TILE
    per_gap = -(-len(copies) // n_gaps)

    def start_some():
        for c in copies[:per_gap]:
            c.start()
        del copies[:per_gap]

    acts = []
    for c0 in range(0, D_FF, MXU_TILE):
        gate = _dot(h, wg_ref[:, c0:c0 + MXU_TILE])
        start_some()
        up = _dot(h, wu_ref[:, c0:c0 + MXU_TILE])
        start_some()
        acts.append(_silu_gate(gate, up))
    act = jnp.concatenate(acts, axis=1)
    outs = []
    for n0 in range(0, D_MODEL, MXU_TILE):
        outs.append(_dot(act, wd_ref[:, n0:n0 + MXU_TILE]))
        start_some()
    return jnp.concatenate(outs, axis=1)


def _dense_ffn_kernel(x_ref, h_ref, wg_ref, wu_ref, wd_ref, o_ref):
    o_ref[...] = x_ref[...] + _swiglu_rows(h_ref[...], wg_ref, wu_ref, wd_ref)


def _dense_ffn(x1, h, wg, wu, wd):
    row = lambda: pl.BlockSpec((ROW_TILE, D_MODEL), lambda i: (i, 0))
    once = pl.Buffered(1)
    return pl.pallas_call(
        _dense_ffn_kernel,
        grid=(N_TOK // ROW_TILE,),
        in_specs=[
            row(), row(),
            pl.BlockSpec((D_MODEL, D_FF), lambda i: (0, 0), pipeline_mode=once),
            pl.BlockSpec((D_MODEL, D_FF), lambda i: (0, 0), pipeline_mode=once),
            pl.BlockSpec((D_FF, D_MODEL), lambda i: (0, 0), pipeline_mode=once),
        ],
        out_specs=row(),
        out_shape=jax.ShapeDtypeStruct((N_TOK, D_MODEL), F32),
        compiler_params=pltpu.CompilerParams(
            dimension_semantics=("parallel",), vmem_limit_bytes=VMEM_LIMIT),
        name="dense_swiglu",
    )(x1, h, wg, wu, wd)


def _row_copy(src_ref, src_row, dst_ref, dst_row, sem):
    return pltpu.make_async_copy(src_ref.at[pl.ds(src_row, 1), :], dst_ref.at[pl.ds(dst_row, 1), :], sem)


def _invert_kernel(dest_ref, pad_lo_ref, pad_hi_ref, src_row_ref, dst_row_ref):
    for e in range(N_EXPERTS + 1):
        def mark_pad(row, carry):
            src_row_ref[row] = 0
            dst_row_ref[row] = N_TOK + (row & (2 * MOE_BLOCK - 1))
            return carry

        lax.fori_loop(pad_lo_ref[e], pad_hi_ref[e], mark_pad, 0)

    def place(token, carry):
        row1 = dest_ref[2 * token]
        row2 = dest_ref[2 * token + 1]
        src_row_ref[row1] = token
        src_row_ref[row2] = token
        dst_row_ref[row1] = token
        dst_row_ref[row2] = Y_PLANE + token
        return carry

    lax.fori_loop(0, N_TOK, place, 0, unroll=8)


def _invert(dest_flat, pad_lo, pad_hi):
    return pl.pallas_call(
        _invert_kernel,
        grid_spec=pltpu.PrefetchScalarGridSpec(
            num_scalar_prefetch=3,
            grid=(1,),
            in_specs=[],
            out_specs=[pl.BlockSpec(memory_space=pltpu.SMEM)] * 2,
        ),
        out_shape=[jax.ShapeDtypeStruct((MOE_CAP,), jnp.int32)] * 2,
        name="moe_invert",
    )(dest_flat, pad_lo, pad_hi)


def _expert_kernel(block_e_ref, src_row_ref, dst_row_ref, h_ref, wg_ref, wu_ref, wd_ref, y_ref,
                   x_buf, y_buf, gather_sem, scatter_sem):
    del block_e_ref
    i = pl.program_id(0)
    last = N_MOE_BLOCKS - 1
    cur = lax.rem(i, RING)
    other = lax.rem(i + 2, RING)

    def gather(block, buf):
        return [_row_copy(h_ref, src_row_ref[block * MOE_BLOCK + r], x_buf.at[buf], r, gather_sem.at[buf])
                for r in range(MOE_BLOCK)]

    def scatter(block, buf, to_dump=None):
        copies = []
        for r in range(MOE_BLOCK):
            row = dst_row_ref[block * MOE_BLOCK + r]
            if to_dump is not None:
                row = jnp.where(to_dump, N_TOK + MOE_BLOCK + r, row)
            copies.append(_row_copy(y_buf.at[buf], r, y_ref, row, scatter_sem.at[buf]))
        return copies

    @pl.when(i == 0)
    def _():
        y_buf[...] = jnp.zeros_like(y_buf)
        for b in range(2):
            for c in gather(b, b):
                c.start()
            dump = pltpu.make_async_copy(
                y_buf.at[b], y_ref.at[pl.ds(N_TOK + b * MOE_BLOCK, MOE_BLOCK), :], scatter_sem.at[b])
            dump.start()
            dump.wait()

    for c in gather(i, cur):
        c.wait()

    @pl.when(i >= 2)
    def _():
        for c in scatter(jnp.maximum(i - 3, 0), cur):
            c.wait()

    copies = (gather(jnp.minimum(i + 2, last), other)
              + scatter(jnp.maximum(i - 1, 0), other, to_dump=(i == 0)))
    y_buf[cur] = _swiglu_rows_spreading(x_buf[cur].astype(BF16), wg_ref, wu_ref, wd_ref, copies)

    @pl.when(i == last)
    def _():
        for c in scatter(last - 2, (last - 2) % RING):
            c.wait()
        for c in scatter(last, last % RING):
            c.start()
        for extra in (last + 1, last + 2):
            for c in gather(last, extra % RING):
                c.wait()
        for block in (last - 1, last):
            for c in scatter(block, block % RING):
                c.wait()


def _experts(block_e, src_row, dst_row, h, wg, wu, wd):
    weights = lambda shape: pl.BlockSpec((None,) + shape, lambda i, be, sr, dr: (be[i], 0, 0))
    return pl.pallas_call(
        _expert_kernel,
        grid_spec=pltpu.PrefetchScalarGridSpec(
            num_scalar_prefetch=3,
            grid=(N_MOE_BLOCKS,),
            in_specs=[pl.BlockSpec(memory_space=pl.ANY),
                      weights((D_MODEL, D_FF)), weights((D_MODEL, D_FF)), weights((D_FF, D_MODEL))],
            out_specs=pl.BlockSpec(memory_space=pl.ANY),
            scratch_shapes=[pltpu.VMEM((RING, MOE_BLOCK, D_MODEL), F32),
                            pltpu.VMEM((RING, MOE_BLOCK, D_MODEL), F32),
                            pltpu.SemaphoreType.DMA((RING,)),
                            pltpu.SemaphoreType.DMA((RING,))],
        ),
        out_shape=jax.ShapeDtypeStruct((Y_PLANE + N_TOK, D_MODEL), F32),
        compiler_params=pltpu.CompilerParams(
            dimension_semantics=("arbitrary",), vmem_limit_bytes=VMEM_LIMIT),
        name="moe_experts",
    )(block_e, src_row, dst_row, h, wg, wu, wd)


def _combine_kernel(x_ref, gates_ref, g_ref, y1_ref, y2_ref, o_ref):
    gates = gates_ref[...]
    g1 = gates[:, 0:1]
    g2 = gates[:, 1:2]
    x2 = x_ref[...] + (y1_ref[...] * g1 + y2_ref[...] * g2)
    o_ref[...] = _rms(x2, g_ref[...])


def _combine_final_norm(x1, gates, g_final, y):
    row = lambda width: pl.BlockSpec((ROW_TILE, width), lambda i: (i, 0))
    return pl.pallas_call(
        _combine_kernel,
        grid=(N_TOK // ROW_TILE,),
        in_specs=[row(D_MODEL), row(2), pl.BlockSpec((1, D_MODEL), lambda i: (0, 0)),
                  row(D_MODEL),
                  pl.BlockSpec((ROW_TILE, D_MODEL), lambda i: (Y_PLANE // ROW_TILE + i, 0))],
        out_specs=row(D_MODEL),
        out_shape=jax.ShapeDtypeStruct((N_TOK, D_MODEL), F32),
        compiler_params=pltpu.CompilerParams(
            dimension_semantics=("parallel",), vmem_limit_bytes=VMEM_LIMIT),
        name="moe_combine_norm",
    )(x1, gates, g_final, y, y)


def _final_norm_kernel(x_ref, g_ref, o_ref):
    o_ref[...] = _rms(x_ref[...], g_ref[...])


def _final_norm(x, g):
    return pl.pallas_call(
        _final_norm_kernel,
        grid=(N_TOK // ROW_TILE,),
        in_specs=[pl.BlockSpec((ROW_TILE, D_MODEL), lambda i: (i, 0)),
                  pl.BlockSpec((1, D_MODEL), lambda i: (0, 0))],
        out_specs=pl.BlockSpec((ROW_TILE, D_MODEL), lambda i: (i, 0)),
        out_shape=jax.ShapeDtypeStruct((N_TOK, D_MODEL), F32),
        compiler_params=pltpu.CompilerParams(dimension_semantics=("parallel",)),
        name="final_norm",
    )(x, g)


def _expert_row_plan(route, counts):
    experts = route[_R_E1:_R_E2 + 1].T.astype(jnp.int32)
    ranks = route[_R_RANK1:_R_RANK2 + 1].T.astype(jnp.int32)
    counts = counts[:, 0].astype(jnp.int32)
    padded = ((counts + MOE_BLOCK - 1) // MOE_BLOCK) * MOE_BLOCK
    ends = jnp.cumsum(padded)
    starts = ends - padded
    start_of = jnp.sum(jnp.where(experts[..., None] == jnp.arange(N_EXPERTS), starts, 0), axis=-1)
    dest = (start_of + ranks).reshape(-1)
    block_row = jnp.arange(N_MOE_BLOCKS, dtype=jnp.int32) * MOE_BLOCK
    block_e = jnp.minimum(jnp.sum(block_row[:, None] >= ends[None, :], axis=-1), N_EXPERTS - 1)
    pad_lo = jnp.concatenate([starts + counts, ends[-1:]])
    pad_hi = jnp.concatenate([ends, jnp.full((1,), MOE_CAP, jnp.int32)])
    return dest.astype(jnp.int32), block_e.astype(jnp.int32), pad_lo.astype(jnp.int32), pad_hi.astype(jnp.int32)


def kernel(x, g_mix, w_in, sgu_ln_g, sgu_w, sgu_b, conv_w, w_out, g_ffn, dense_w_gate, dense_w_up,
           dense_w_down, router_w, moe_w_gate, moe_w_up, moe_w_down, g_final):
    x = x.reshape(N_TOK, D_MODEL)
    bias_table = _dilated_bias_table()
    out = None
    for l in range(DEPTH):
        za, zb, zc, zd = _norm_inproj(x, g_mix[l].reshape(1, D_MODEL), w_in[l].astype(BF16))
        ya = _moba(za)
        yd = _dilated(zd, bias_table)
        yb, yc = _sgu_conv(zb, zc, sgu_ln_g[l], sgu_w[l], sgu_b[l], conv_w[l])
        ys = (ya, yb, yc, yd)
        g2 = g_ffn[l].reshape(1, D_MODEL)
        j = l // 2
        if l % 2 == 0:
            x1, h = _outproj(x, ys, w_out[l].astype(BF16), g2)
            x = _dense_ffn(x1, h, dense_w_gate[j].astype(BF16), dense_w_up[j].astype(BF16),
                           dense_w_down[j].astype(BF16))
            if l == DEPTH - 1:
                out = _final_norm(x, g_final.reshape(1, D_MODEL))
        else:
            wr_hi, wr_lo = _split_bf16(router_w[j].T)
            router = (jnp.concatenate([wr_hi, wr_lo]), jnp.concatenate([wr_hi, jnp.zeros_like(wr_hi)]))
            x1, h, route, counts = _outproj(x, ys, w_out[l].astype(BF16), g2, router=router)
            dest, block_e, pad_lo, pad_hi = _expert_row_plan(route, counts)
            src_row, dst_row = _invert(dest, pad_lo, pad_hi)
            y = _experts(block_e, src_row, dst_row, h, moe_w_gate[j].astype(BF16), moe_w_up[j].astype(BF16),
                         moe_w_down[j].astype(BF16))
            if l == DEPTH - 1:
                out = _combine_final_norm(x1, route[_R_G1:_R_G2 + 1].T, g_final.reshape(1, D_MODEL), y)
            else:
                raise NotImplementedError("an expert layer that is not the last layer")
    return out.reshape(BATCH, SEQ, D_MODEL)
```

```python
import functools

import numpy as np
import jax
import jax.numpy as jnp
from jax import lax
from jax.experimental import pallas as pl
from jax.experimental.pallas import tpu as pltpu

D_MODEL = 1024
BATCH = 8
SEQ = 2048
DEPTH = 2
N_TOK = BATCH * SEQ
HEAD_DIM = 64
N_HEADS = 4
W_MIX = N_HEADS * HEAD_DIM
CUT_A = 3 * W_MIX
CUT_B = CUT_A + 2 * W_MIX
CUT_C = CUT_B + 3 * W_MIX
IN_COLS = CUT_C + 3 * W_MIX
ATT_BLOCK = 256
HEAD_ROWS = HEAD_DIM + 16
N_ATT_BLOCKS = SEQ // ATT_BLOCK
MOBA_TOPK = 3
SGU_CHUNK = 128
DILATED_PATTERNS = ((128, 1), (512, 4), (2048, 16))
D_FF = 2816
FF_CHUNK = D_FF // 2
N_EXPERTS = 8
MOE_BLOCK = 256
N_MOE_BLOCKS = (N_TOK * 2) // MOE_BLOCK + N_EXPERTS
MOE_CAP = N_MOE_BLOCKS * MOE_BLOCK
Y_PLANE = N_TOK + 2 * MOE_BLOCK
RING = 3
RMS_EPS = 1e-6
LN_EPS = 1e-5
ATTN_SCALE = HEAD_DIM ** -0.5
MASK_BIAS = -1e30

LANES = 128
MXU_TILE = 256
ROW_TILE = 512
VMEM_LIMIT = 56 * 1024 * 1024

F32 = jnp.float32
BF16 = jnp.bfloat16


def _dot(a, b):
    return jnp.dot(a, b, preferred_element_type=F32)


def _dot_nt(a, b):
    return lax.dot_general(a, b, (((1,), (1,)), ((), ())), preferred_element_type=F32)


def _rms(x, g):
    return x * lax.rsqrt(jnp.mean(x * x, axis=-1, keepdims=True) + RMS_EPS) * g


def _cast_weight_once(w_ref, w16_ref):
    @pl.when(pl.program_id(0) == 0)
    def _():
        for c0 in range(0, w_ref.shape[1], MXU_TILE):
            w16_ref[:, c0:c0 + MXU_TILE] = w_ref[:, c0:c0 + MXU_TILE].astype(BF16)


def _split_bf16(x):
    hi = x.astype(BF16)
    lo = (x - hi.astype(F32)).astype(BF16)
    return hi, lo


def _norm_inproj_kernel(x_ref, g_ref, w32_ref, za_ref, zb_ref, zc_ref, zd_ref, w_ref):
    _cast_weight_once(w32_ref, w_ref)
    h = _rms(x_ref[...], g_ref[...]).astype(BF16)
    za_ref[...] = _dot(h, w_ref[:, 0:CUT_A]).astype(BF16)
    zb_ref[...] = _dot(h, w_ref[:, CUT_A:CUT_B]).astype(BF16)
    zc_ref[...] = _dot(h, w_ref[:, CUT_B:CUT_C]).astype(BF16)
    zd_ref[...] = _dot(h, w_ref[:, CUT_C:IN_COLS]).astype(BF16)


def _norm_inproj(x, g, w):
    widths = (CUT_A, CUT_B - CUT_A, CUT_C - CUT_B, IN_COLS - CUT_C)
    return pl.pallas_call(
        _norm_inproj_kernel,
        grid=(N_TOK // ROW_TILE,),
        in_specs=[
            pl.BlockSpec((ROW_TILE, D_MODEL), lambda i: (i, 0)),
            pl.BlockSpec((1, D_MODEL), lambda i: (0, 0)),
            pl.BlockSpec((D_MODEL, IN_COLS), lambda i: (0, 0), pipeline_mode=pl.Buffered(1)),
        ],
        out_specs=[pl.BlockSpec((ROW_TILE, w_), lambda i: (i, 0)) for w_ in widths],
        out_shape=[jax.ShapeDtypeStruct((N_TOK, w_), BF16) for w_ in widths],
        scratch_shapes=[pltpu.VMEM((D_MODEL, IN_COLS), BF16)],
        compiler_params=pltpu.CompilerParams(
            dimension_semantics=("arbitrary",), vmem_limit_bytes=VMEM_LIMIT),
        name="norm_inproj",
    )(x, g, w)


def _head_lane_mask(h, width):
    lane = lax.broadcasted_iota(jnp.int32, (ATT_BLOCK, width), 1)
    return (lane >= h * HEAD_DIM) & (lane < (h + 1) * HEAD_DIM)


def _store_transposed_v(z_ref, vt_ref):
    row = lax.broadcasted_iota(jnp.int32, (HEAD_ROWS - HEAD_DIM, ATT_BLOCK), 0)
    ones_row = jnp.where(row == 0, 1.0, 0.0).astype(BF16)
    for j in range(N_ATT_BLOCKS):
        vt = z_ref[j * ATT_BLOCK:(j + 1) * ATT_BLOCK, 2 * W_MIX:3 * W_MIX].astype(F32).T
        for h in range(N_HEADS):
            vt_ref[j, h * HEAD_ROWS:h * HEAD_ROWS + HEAD_DIM, :] = (
                vt[h * HEAD_DIM:(h + 1) * HEAD_DIM].astype(BF16))
            vt_ref[j, h * HEAD_ROWS + HEAD_DIM:(h + 1) * HEAD_ROWS, :] = ones_row


def _stacked_queries(z_ref, row0):
    qs = z_ref[pl.ds(row0, ATT_BLOCK), 0:W_MIX].astype(F32) * ATTN_SCALE
    return jnp.concatenate(
        [jnp.where(_head_lane_mask(h, W_MIX), qs, 0.0) for h in range(N_HEADS)], axis=0).astype(BF16)


def _head_cols(h):
    return slice(h * ATT_BLOCK, (h + 1) * ATT_BLOCK)


def _values_times_probs(vt_ref, blk, p):
    return jnp.concatenate(
        [_dot(vt_ref[blk, h * HEAD_ROWS:(h + 1) * HEAD_ROWS, :], p[:, _head_cols(h)]) for h in range(N_HEADS)],
        axis=0)


def _per_head_rows(a):
    return jnp.concatenate(
        [jnp.broadcast_to(a[:, _head_cols(h)], (HEAD_ROWS, ATT_BLOCK)) for h in range(N_HEADS)], axis=0)


def _softmax_first(s, vt_ref, blk):
    m = jnp.max(s, axis=0, keepdims=True)
    return m, _values_times_probs(vt_ref, blk, jnp.exp(s - m).astype(BF16))


def _softmax_step(s, bias_row, vt_ref, blk, state):
    m, acc = state
    m_new = jnp.maximum(m, jnp.max(s, axis=0, keepdims=True) + bias_row)
    alpha = jnp.exp(m - m_new)
    p = jnp.exp(s - (m_new - bias_row))
    acc = _per_head_rows(alpha) * acc + _values_times_probs(vt_ref, blk, p.astype(BF16))
    return m_new, acc


def _attend(i, own_scores, scores, bias_row, vt_ref, s_refs, o_ref):
    s_a, s_b = s_refs
    last = jnp.maximum(i - 1, 0)
    s_a[...] = own_scores
    s_b[...] = scores(0)
    state = _softmax_first(s_a[...], vt_ref, i)
    s_a[...] = scores(jnp.minimum(1, last))
    state = _softmax_step(s_b[...], bias_row(0), vt_ref, 0, state)

    def two_blocks(t, state):
        j0 = 2 * t + 1
        j1 = j0 + 1
        s_b[...] = scores(jnp.minimum(j1, last))
        state = _softmax_step(s_a[...], bias_row(j0), vt_ref, j0, state)
        s_a[...] = scores(jnp.minimum(j1 + 1, last))
        return _softmax_step(s_b[...], bias_row(j1), vt_ref, j1, state)

    _, acc = lax.fori_loop(0, i // 2, two_blocks, state)
    heads = []
    for h in range(N_HEADS):
        r0 = h * HEAD_ROWS
        heads.append(acc[r0:r0 + HEAD_DIM] / acc[r0 + HEAD_DIM:r0 + HEAD_DIM + 1])
    o_ref[...] = jnp.concatenate(heads, axis=0).T.astype(o_ref.dtype)


def _moba_kernel(z_ref, o_ref, kmt_hi_ref, kmt_lo_ref, vt_ref, bias_ref, sa_ref, sb_ref):
    i = pl.program_id(1)

    @pl.when(i == 0)
    def _prepare_batch():
        means = []
        for j in range(N_ATT_BLOCKS):
            kj = z_ref[j * ATT_BLOCK:(j + 1) * ATT_BLOCK, W_MIX:2 * W_MIX].astype(F32)
            means.append(jnp.sum(kj, axis=0, keepdims=True) * (1.0 / ATT_BLOCK))
        mt = jnp.concatenate(means * N_HEADS, axis=0)
        r = lax.broadcasted_iota(jnp.int32, mt.shape, 0)
        c = lax.broadcasted_iota(jnp.int32, mt.shape, 1)
        hi, lo = _split_bf16(jnp.where((c >> 6) == (r >> 3), mt, 0.0))
        kmt_hi_ref[...] = hi
        kmt_lo_ref[...] = lo
        _store_transposed_v(z_ref, vt_ref)

    row0 = pl.multiple_of(i * ATT_BLOCK, ATT_BLOCK)
    q = z_ref[pl.ds(row0, ATT_BLOCK), 0:W_MIX]
    q_stack = _stacked_queries(z_ref, row0)

    def scores(blk):
        r0 = pl.multiple_of(blk * ATT_BLOCK, ATT_BLOCK)
        return _dot_nt(z_ref[pl.ds(r0, ATT_BLOCK), W_MIX:2 * W_MIX], q_stack)

    gates = _dot_nt(kmt_hi_ref[...], q) + _dot_nt(kmt_lo_ref[...], q)
    blk = lax.broadcasted_iota(jnp.int32, (N_ATT_BLOCKS, ATT_BLOCK), 0)
    past = blk < i
    biases = []
    for h in range(N_HEADS):
        g = jnp.where(past, gates[h * N_ATT_BLOCKS:(h + 1) * N_ATT_BLOCKS], -jnp.inf)
        rank = jnp.zeros(g.shape, F32)
        for d in range(1, N_ATT_BLOCKS):
            lower = pltpu.roll(g, d, axis=0)
            rank = rank + jnp.where((blk >= d) & (lower >= g), 1.0, 0.0)
            upper = pltpu.roll(g, N_ATT_BLOCKS - d, axis=0)
            rank = rank + jnp.where((blk + d < N_ATT_BLOCKS) & (upper > g), 1.0, 0.0)
        biases.append(jnp.where(past & (rank < MOBA_TOPK), 0.0, MASK_BIAS))
    bias = jnp.concatenate(biases, axis=1)
    for j in range(N_ATT_BLOCKS):
        bias_ref[j] = bias[j:j + 1]

    ki = lax.broadcasted_iota(jnp.int32, (ATT_BLOCK, N_HEADS * ATT_BLOCK), 0)
    qi = lax.broadcasted_iota(jnp.int32, (ATT_BLOCK, N_HEADS * ATT_BLOCK), 1) & (ATT_BLOCK - 1)
    own = jnp.where(ki <= qi, scores(i), -jnp.inf)
    _attend(i, own, scores, lambda j: bias_ref[j], vt_ref, (sa_ref, sb_ref), o_ref)


def _moba(za):
    za3 = za.reshape(BATCH, SEQ, CUT_A)
    out = pl.pallas_call(
        _moba_kernel,
        grid=(BATCH, N_ATT_BLOCKS),
        in_specs=[pl.BlockSpec((None, SEQ, CUT_A), lambda b, i: (b, 0, 0))],
        out_specs=pl.BlockSpec((None, ATT_BLOCK, W_MIX), lambda b, i: (b, i, 0)),
        out_shape=jax.ShapeDtypeStruct((BATCH, SEQ, W_MIX), BF16),
        scratch_shapes=[
            pltpu.VMEM((N_HEADS * N_ATT_BLOCKS, W_MIX), BF16),
            pltpu.VMEM((N_HEADS * N_ATT_BLOCKS, W_MIX), BF16),
            pltpu.VMEM((N_ATT_BLOCKS, N_HEADS * HEAD_ROWS, ATT_BLOCK), BF16),
            pltpu.VMEM((N_ATT_BLOCKS, 1, N_HEADS * ATT_BLOCK), F32),
            pltpu.VMEM((ATT_BLOCK, N_HEADS * ATT_BLOCK), F32),
            pltpu.VMEM((ATT_BLOCK, N_HEADS * ATT_BLOCK), F32),
        ],
        compiler_params=pltpu.CompilerParams(
            dimension_semantics=("parallel", "arbitrary"), vmem_limit_bytes=VMEM_LIMIT),
        name="moba_attention",
    )(za3)
    return out.reshape(N_TOK, W_MIX)


def _dilated_bias_table():
    d = np.arange(ATT_BLOCK)[None, :] - np.arange(ATT_BLOCK)[:, None]
    tiles = []
    for k in range(N_ATT_BLOCKS):
        dist = d + k * ATT_BLOCK
        count = np.zeros(dist.shape, np.float64)
        for window, dil in DILATED_PATTERNS:
            count += (dist >= 0) & (dist <= window) & (dist % dil == 0)
        with np.errstate(divide="ignore"):
            tiles.append(np.log(count))
    return jnp.asarray(np.stack(tiles), F32)


def _dilated_kernel(z_ref, bias_ref, o_ref, vt_ref, sa_ref, sb_ref):
    i = pl.program_id(1)

    @pl.when(i == 0)
    def _prepare_batch():
        _store_transposed_v(z_ref, vt_ref)

    row0 = pl.multiple_of(i * ATT_BLOCK, ATT_BLOCK)
    q_stack = _stacked_queries(z_ref, row0)

    def scores(blk):
        r0 = pl.multiple_of(blk * ATT_BLOCK, ATT_BLOCK)
        s = _dot_nt(z_ref[pl.ds(r0, ATT_BLOCK), W_MIX:2 * W_MIX], q_stack)
        b = bias_ref[i - blk]
        return jnp.concatenate([s[:, _head_cols(h)] + b for h in range(N_HEADS)], axis=1)

    def bias_row(j):
        return jnp.full((1, N_HEADS * ATT_BLOCK), jnp.where(j < i, 0.0, MASK_BIAS), F32)

    _attend(i, scores(i), scores, bias_row, vt_ref, (sa_ref, sb_ref), o_ref)


def _dilated(zd, bias_table):
    zd3 = zd.reshape(BATCH, SEQ, 3 * W_MIX)
    out = pl.pallas_call(
        _dilated_kernel,
        grid=(BATCH, N_ATT_BLOCKS),
        in_specs=[
            pl.BlockSpec((None, SEQ, 3 * W_MIX), lambda b, i: (b, 0, 0)),
            pl.BlockSpec((N_ATT_BLOCKS, ATT_BLOCK, ATT_BLOCK), lambda b, i: (0, 0, 0)),
        ],
        out_specs=pl.BlockSpec((None, ATT_BLOCK, W_MIX), lambda b, i: (b, i, 0)),
        out_shape=jax.ShapeDtypeStruct((BATCH, SEQ, W_MIX), BF16),
        scratch_shapes=[pltpu.VMEM((N_ATT_BLOCKS, N_HEADS * HEAD_ROWS, ATT_BLOCK), BF16),
                        pltpu.VMEM((ATT_BLOCK, N_HEADS * ATT_BLOCK), F32),
                        pltpu.VMEM((ATT_BLOCK, N_HEADS * ATT_BLOCK), F32)],
        compiler_params=pltpu.CompilerParams(
            dimension_semantics=("parallel", "arbitrary"), vmem_limit_bytes=VMEM_LIMIT),
        name="dilated_attention",
    )(zd3, bias_table)
    return out.reshape(N_TOK, W_MIX)


def _gelu_tanh(x):
    return 0.5 * x * (1.0 + jnp.tanh(np.sqrt(2.0 / np.pi).astype(np.float32) * (x + 0.044715 * (x * x * x))))


def _sgu_conv_kernel(zb_ref, zc_ref, lng_ref, ws_ref, bs_ref, cw_ref, yb_ref, yc_ref):
    r = lax.broadcasted_iota(jnp.int32, (N_HEADS * SGU_CHUNK, SGU_CHUNK), 0)
    c = lax.broadcasted_iota(jnp.int32, (N_HEADS * SGU_CHUNK, SGU_CHUNK), 1)
    w_stack = jnp.where((r & (SGU_CHUNK - 1)) >= c, ws_ref[...], 0.0).astype(BF16)
    group = lax.broadcasted_iota(jnp.int32, (SGU_CHUNK, W_MIX), 1) >> 6
    ln_g = lng_ref[...]
    b_tile = bs_ref[...]

    def chunk(n, carry):
        r0 = pl.multiple_of(n * SGU_CHUNK, SGU_CHUNK)
        gz = _gelu_tanh(zb_ref[pl.ds(r0, SGU_CHUNK), :].astype(F32))
        u = gz[:, 0:W_MIX]
        v = gz[:, W_MIX:2 * W_MIX]
        vc = v - jnp.mean(v, axis=-1, keepdims=True)
        vn = vc * lax.rsqrt(jnp.mean(vc * vc, axis=-1, keepdims=True) + LN_EPS) * ln_g
        mixed = _dot(w_stack, vn.astype(BF16))
        sv = mixed[0:SGU_CHUNK]
        for gi in range(1, N_HEADS):
            sv = jnp.where(group == gi, mixed[gi * SGU_CHUNK:(gi + 1) * SGU_CHUNK], sv)
        yb_ref[pl.ds(r0, SGU_CHUNK), :] = (u * (sv + b_tile)).astype(yb_ref.dtype)
        return carry

    lax.fori_loop(0, SEQ // SGU_CHUNK, chunk, 0, unroll=4)

    zc = zc_ref[...].astype(F32)
    z = zc[:, W_MIX:2 * W_MIX] * zc[:, 2 * W_MIX:3 * W_MIX]
    t = lax.broadcasted_iota(jnp.int32, (SEQ, W_MIX), 0)
    z1 = jnp.where(t >= 1, pltpu.roll(z, 1, axis=0), 0.0)
    z2 = jnp.where(t >= 2, pltpu.roll(z, 2, axis=0), 0.0)
    y = z2 * cw_ref[0:1, :]
    y = y + z1 * cw_ref[1:2, :]
    y = y + z * cw_ref[2:3, :]
    yc_ref[...] = (zc[:, 0:W_MIX] * y).astype(yc_ref.dtype)


def _sgu_conv(zb, zc, ln_g, w_s, b_s, conv_w):
    w_stack = w_s.reshape(N_HEADS * SGU_CHUNK, SGU_CHUNK)
    b_tile = jnp.repeat(b_s.T, HEAD_DIM, axis=1)
    yb, yc = pl.pallas_call(
        _sgu_conv_kernel,
        grid=(BATCH,),
        in_specs=[
            pl.BlockSpec((None, SEQ, 2 * W_MIX), lambda b: (b, 0, 0)),
            pl.BlockSpec((None, SEQ, 3 * W_MIX), lambda b: (b, 0, 0)),
            pl.BlockSpec((1, W_MIX), lambda b: (0, 0)),
            pl.BlockSpec((N_HEADS * SGU_CHUNK, SGU_CHUNK), lambda b: (0, 0)),
            pl.BlockSpec((SGU_CHUNK, W_MIX), lambda b: (0, 0)),
            pl.BlockSpec((3, W_MIX), lambda b: (0, 0)),
        ],
        out_specs=[pl.BlockSpec((None, SEQ, W_MIX), lambda b: (b, 0, 0))] * 2,
        out_shape=[jax.ShapeDtypeStruct((BATCH, SEQ, W_MIX), BF16)] * 2,
        compiler_params=pltpu.CompilerParams(
            dimension_semantics=("parallel",), vmem_limit_bytes=VMEM_LIMIT),
        name="sgu_conv",
    )(zb.reshape(BATCH, SEQ, 2 * W_MIX), zc.reshape(BATCH, SEQ, 3 * W_MIX),
      ln_g.reshape(1, W_MIX), w_stack, b_tile, conv_w)
    return yb.reshape(N_TOK, W_MIX), yc.reshape(N_TOK, W_MIX)


def _outproj_residual(x_ref, ya_ref, yb_ref, yc_ref, yd_ref, w_ref):
    acc = _dot(ya_ref[...], w_ref[0:W_MIX, :])
    acc = acc + _dot(yb_ref[...], w_ref[W_MIX:2 * W_MIX, :])
    acc = acc + _dot(yc_ref[...], w_ref[2 * W_MIX:3 * W_MIX, :])
    acc = acc + _dot(yd_ref[...], w_ref[3 * W_MIX:4 * W_MIX, :])
    return x_ref[...] + acc


def _outproj_dense_kernel(x_ref, ya_ref, yb_ref, yc_ref, yd_ref, w32_ref, g_ref, x1_ref, h_ref, w_ref):
    _cast_weight_once(w32_ref, w_ref)
    x1 = _outproj_residual(x_ref, ya_ref, yb_ref, yc_ref, yd_ref, w_ref)
    x1_ref[...] = x1
    h_ref[...] = _rms(x1, g_ref[...]).astype(BF16)


_R_E1, _R_E2, _R_G1, _R_G2, _R_RANK1, _R_RANK2 = range(6)


def _outproj_router_kernel(x_ref, ya_ref, yb_ref, yc_ref, yd_ref, w32_ref, g_ref, wra_ref, wrb_ref,
                           x1_ref, h_ref, route_ref, counts_ref, running_ref, w_ref):
    step = pl.program_id(0)
    _cast_weight_once(w32_ref, w_ref)

    @pl.when(step == 0)
    def _():
        running_ref[...] = jnp.zeros_like(running_ref)

    x1 = _outproj_residual(x_ref, ya_ref, yb_ref, yc_ref, yd_ref, w_ref)
    x1_ref[...] = x1
    h = _rms(x1, g_ref[...])
    h_ref[...] = h

    h_hi, h_lo = _split_bf16(h)
    part = _dot_nt(wra_ref[...], h_hi)
    logits = part[0:N_EXPERTS] + (part[N_EXPERTS:2 * N_EXPERTS] + _dot_nt(wrb_ref[...], h_lo)[0:N_EXPERTS])
    expert = lax.broadcasted_iota(jnp.int32, (N_EXPERTS, ROW_TILE), 0).astype(F32)
    m1 = jnp.max(logits, axis=0, keepdims=True)
    e1 = jnp.min(jnp.where(logits == m1, expert, float(N_EXPERTS)), axis=0, keepdims=True)
    rest = jnp.where(expert == e1, -jnp.inf, logits)
    m2 = jnp.max(rest, axis=0, keepdims=True)
    e2 = jnp.min(jnp.where(rest == m2, expert, float(N_EXPERTS)), axis=0, keepdims=True)
    t = jnp.exp(m2 - m1)
    g1 = 1.0 / (1.0 + t)
    g2 = t / (1.0 + t)

    chosen = jnp.where((expert == e1) | (expert == e2), 1.0, 0.0)
    ri = lax.broadcasted_iota(jnp.int32, (ROW_TILE, ROW_TILE), 0)
    ci = lax.broadcasted_iota(jnp.int32, (ROW_TILE, ROW_TILE), 1)
    earlier = jnp.where(ri < ci, 1.0, 0.0).astype(BF16)
    chosen16 = jnp.concatenate([chosen, jnp.zeros_like(chosen)], axis=0).astype(BF16)
    before = _dot(chosen16, earlier)[0:N_EXPERTS] + running_ref[:, 0:1]
    rank1 = jnp.sum(jnp.where(expert == e1, before, 0.0), axis=0, keepdims=True)
    rank2 = jnp.sum(jnp.where(expert == e2, before, 0.0), axis=0, keepdims=True)
    running_ref[...] = running_ref[...] + jnp.sum(chosen, axis=1, keepdims=True)
    counts_ref[...] = running_ref[...]

    rec = jnp.zeros((N_EXPERTS, ROW_TILE), F32)
    for pos, val in ((_R_E1, e1), (_R_E2, e2), (_R_G1, g1), (_R_G2, g2),
                     (_R_RANK1, rank1), (_R_RANK2, rank2)):
        rec = jnp.where(expert == float(pos), val, rec)
    route_ref[...] = rec


def _outproj(x, ys, w, g, router=None):
    row = lambda width: pl.BlockSpec((ROW_TILE, width), lambda i: (i, 0))
    const = lambda shape: pl.BlockSpec(shape, lambda i: (0, 0))
    weight = pl.BlockSpec((D_MODEL, D_MODEL), lambda i: (0, 0), pipeline_mode=pl.Buffered(1))
    w16 = pltpu.VMEM((D_MODEL, D_MODEL), BF16)
    in_specs = [row(D_MODEL)] + [row(W_MIX)] * 4 + [weight, const((1, D_MODEL))]
    if router is None:
        return pl.pallas_call(
            _outproj_dense_kernel,
            grid=(N_TOK // ROW_TILE,),
            in_specs=in_specs,
            out_specs=[row(D_MODEL), row(D_MODEL)],
            out_shape=[jax.ShapeDtypeStruct((N_TOK, D_MODEL), F32),
                       jax.ShapeDtypeStruct((N_TOK, D_MODEL), BF16)],
            scratch_shapes=[w16],
            compiler_params=pltpu.CompilerParams(
                dimension_semantics=("arbitrary",), vmem_limit_bytes=VMEM_LIMIT),
            name="outproj_norm",
        )(x, *ys, w, g)
    wra, wrb = router
    return pl.pallas_call(
        _outproj_router_kernel,
        grid=(N_TOK // ROW_TILE,),
        in_specs=in_specs + [const((2 * N_EXPERTS, D_MODEL)), const((2 * N_EXPERTS, D_MODEL))],
        out_specs=[row(D_MODEL), row(D_MODEL),
                   pl.BlockSpec((N_EXPERTS, ROW_TILE), lambda i: (0, i)), const((N_EXPERTS, LANES))],
        out_shape=[jax.ShapeDtypeStruct((N_TOK, D_MODEL), F32),
                   jax.ShapeDtypeStruct((N_TOK, D_MODEL), F32),
                   jax.ShapeDtypeStruct((N_EXPERTS, N_TOK), F32),
                   jax.ShapeDtypeStruct((N_EXPERTS, LANES), F32)],
        scratch_shapes=[pltpu.VMEM((N_EXPERTS, LANES), F32), w16],
        compiler_params=pltpu.CompilerParams(
            dimension_semantics=("arbitrary",), vmem_limit_bytes=VMEM_LIMIT),
        name="outproj_router",
    )(x, *ys, w, g, wra, wrb)


def _silu_gate(gate, up):
    return (gate * (1.0 / (1.0 + jnp.exp(-gate))) * up).astype(BF16)


def _swiglu_rows(h, wg_ref, wu_ref, wd_ref):
    acc = None
    for c0 in range(0, D_FF, FF_CHUNK):
        act = _silu_gate(_dot(h, wg_ref[:, c0:c0 + FF_CHUNK]), _dot(h, wu_ref[:, c0:c0 + FF_CHUNK]))
        part = _dot(act, wd_ref[c0:c0 + FF_CHUNK, :])
        acc = part if acc is None else acc + part
    return acc


def _swiglu_rows_spreading(h, wg_ref, wu_ref, wd_ref, copies):
    copies = list(copies)
    n_gaps = 2 * (D_FF // MXU_TILE) + D_MODEL // MXU_TILE
    per_gap = -(-len(copies) // n_gaps)

    def start_some():
        for c in copies[:per_gap]:
            c.start()
        del copies[:per_gap]

    acts = []
    for c0 in range(0, D_FF, MXU_TILE):
        gate = _dot(h, wg_ref[:, c0:c0 + MXU_TILE])
        start_some()
        up = _dot(h, wu_ref[:, c0:c0 + MXU_TILE])
        start_some()
        acts.append(_silu_gate(gate, up))
    act = jnp.concatenate(acts, axis=1)
    outs = []
    for n0 in range(0, D_MODEL, MXU_TILE):
        outs.append(_dot(act, wd_ref[:, n0:n0 + MXU_TILE]))
        start_some()
    return jnp.concatenate(outs, axis=1)


def _dense_ffn_kernel(x_ref, h_ref, wg_ref, wu_ref, wd_ref, o_ref):
    o_ref[...] = x_ref[...] + _swiglu_rows(h_ref[...], wg_ref, wu_ref, wd_ref)


def _dense_ffn(x1, h, wg, wu, wd):
    row = lambda: pl.BlockSpec((ROW_TILE, D_MODEL), lambda i: (i, 0))
    once = pl.Buffered(1)
    return pl.pallas_call(
        _dense_ffn_kernel,
        grid=(N_TOK // ROW_TILE,),
        in_specs=[
            row(), row(),
            pl.BlockSpec((D_MODEL, D_FF), lambda i: (0, 0), pipeline_mode=once),
            pl.BlockSpec((D_MODEL, D_FF), lambda i: (0, 0), pipeline_mode=once),
            pl.BlockSpec((D_FF, D_MODEL), lambda i: (0, 0), pipeline_mode=once),
        ],
        out_specs=row(),
        out_shape=jax.ShapeDtypeStruct((N_TOK, D_MODEL), F32),
        compiler_params=pltpu.CompilerParams(
            dimension_semantics=("parallel",), vmem_limit_bytes=VMEM_LIMIT),
        name="dense_swiglu",
    )(x1, h, wg, wu, wd)


def _row_copy(src_ref, src_row, dst_ref, dst_row, sem):
    return pltpu.make_async_copy(src_ref.at[pl.ds(src_row, 1), :], dst_ref.at[pl.ds(dst_row, 1), :], sem)


def _invert_kernel(dest_ref, pad_lo_ref, pad_hi_ref, src_row_ref, dst_row_ref):
    for e in range(N_EXPERTS + 1):
        def mark_pad(row, carry):
            src_row_ref[row] = 0
            dst_row_ref[row] = N_TOK + (row & (2 * MOE_BLOCK - 1))
            return carry

        lax.fori_loop(pad_lo_ref[e], pad_hi_ref[e], mark_pad, 0)

    def place(token, carry):
        row1 = dest_ref[2 * token]
        row2 = dest_ref[2 * token + 1]
        src_row_ref[row1] = token
        src_row_ref[row2] = token
        dst_row_ref[row1] = token
        dst_row_ref[row2] = Y_PLANE + token
        return carry

    lax.fori_loop(0, N_TOK, place, 0, unroll=8)


def _invert(dest_flat, pad_lo, pad_hi):
    return pl.pallas_call(
        _invert_kernel,
        grid_spec=pltpu.PrefetchScalarGridSpec(
            num_scalar_prefetch=3,
            grid=(1,),
            in_specs=[],
            out_specs=[pl.BlockSpec(memory_space=pltpu.SMEM)] * 2,
        ),
        out_shape=[jax.ShapeDtypeStruct((MOE_CAP,), jnp.int32)] * 2,
        name="moe_invert",
    )(dest_flat, pad_lo, pad_hi)


def _expert_kernel(block_e_ref, src_row_ref, dst_row_ref, h_ref, wg_ref, wu_ref, wd_ref, y_ref,
                   x_buf, y_buf, gather_sem, scatter_sem):
    del block_e_ref
    i = pl.program_id(0)
    last = N_MOE_BLOCKS - 1
    cur = lax.rem(i, RING)
    other = lax.rem(i + 2, RING)

    def gather(block, buf):
        return [_row_copy(h_ref, src_row_ref[block * MOE_BLOCK + r], x_buf.at[buf], r, gather_sem.at[buf])
                for r in range(MOE_BLOCK)]

    def scatter(block, buf, to_dump=None):
        copies = []
        for r in range(MOE_BLOCK):
            row = dst_row_ref[block * MOE_BLOCK + r]
            if to_dump is not None:
                row = jnp.where(to_dump, N_TOK + MOE_BLOCK + r, row)
            copies.append(_row_copy(y_buf.at[buf], r, y_ref, row, scatter_sem.at[buf]))
        return copies

    @pl.when(i == 0)
    def _():
        y_buf[...] = jnp.zeros_like(y_buf)
        for b in range(2):
            for c in gather(b, b):
                c.start()
            dump = pltpu.make_async_copy(
                y_buf.at[b], y_ref.at[pl.ds(N_TOK + b * MOE_BLOCK, MOE_BLOCK), :], scatter_sem.at[b])
            dump.start()
            dump.wait()

    for c in gather(i, cur):
        c.wait()

    @pl.when(i >= 2)
    def _():
        for c in scatter(jnp.maximum(i - 3, 0), cur):
            c.wait()

    copies = (gather(jnp.minimum(i + 2, last), other)
              + scatter(jnp.maximum(i - 1, 0), other, to_dump=(i == 0)))
    y_buf[cur] = _swiglu_rows_spreading(x_buf[cur].astype(BF16), wg_ref, wu_ref, wd_ref, copies)

    @pl.when(i == last)
    def _():
        for c in scatter(last - 2, (last - 2) % RING):
            c.wait()
        for c in scatter(last, last % RING):
            c.start()
        for extra in (last + 1, last + 2):
            for c in gather(last, extra % RING):
                c.wait()
        for block in (last - 1, last):
            for c in scatter(block, block % RING):
                c.wait()


def _experts(block_e, src_row, dst_row, h, wg, wu, wd):
    weights = lambda shape: pl.BlockSpec((None,) + shape, lambda i, be, sr, dr: (be[i], 0, 0))
    return pl.pallas_call(
        _expert_kernel,
        grid_spec=pltpu.PrefetchScalarGridSpec(
            num_scalar_prefetch=3,
            grid=(N_MOE_BLOCKS,),
            in_specs=[pl.BlockSpec(memory_space=pl.ANY),
                      weights((D_MODEL, D_FF)), weights((D_MODEL, D_FF)), weights((D_FF, D_MODEL))],
            out_specs=pl.BlockSpec(memory_space=pl.ANY),
            scratch_shapes=[pltpu.VMEM((RING, MOE_BLOCK, D_MODEL), F32),
                            pltpu.VMEM((RING, MOE_BLOCK, D_MODEL), F32),
                            pltpu.SemaphoreType.DMA((RING,)),
                            pltpu.SemaphoreType.DMA((RING,))],
        ),
        out_shape=jax.ShapeDtypeStruct((Y_PLANE + N_TOK, D_MODEL), F32),
        compiler_params=pltpu.CompilerParams(
            dimension_semantics=("arbitrary",), vmem_limit_bytes=VMEM_LIMIT),
        name="moe_experts",
    )(block_e, src_row, dst_row, h, wg, wu, wd)


def _combine_kernel(x_ref, gates_ref, g_ref, y1_ref, y2_ref, o_ref):
    gates = gates_ref[...]
    g1 = gates[:, 0:1]
    g2 = gates[:, 1:2]
    x2 = x_ref[...] + (y1_ref[...] * g1 + y2_ref[...] * g2)
    o_ref[...] = _rms(x2, g_ref[...])


def _combine_final_norm(x1, gates, g_final, y):
    row = lambda width: pl.BlockSpec((ROW_TILE, width), lambda i: (i, 0))
    return pl.pallas_call(
        _combine_kernel,
        grid=(N_TOK // ROW_TILE,),
        in_specs=[row(D_MODEL), row(2), pl.BlockSpec((1, D_MODEL), lambda i: (0, 0)),
                  row(D_MODEL),
                  pl.BlockSpec((ROW_TILE, D_MODEL), lambda i: (Y_PLANE // ROW_TILE + i, 0))],
        out_specs=row(D_MODEL),
        out_shape=jax.ShapeDtypeStruct((N_TOK, D_MODEL), F32),
        compiler_params=pltpu.CompilerParams(
            dimension_semantics=("parallel",), vmem_limit_bytes=VMEM_LIMIT),
        name="moe_combine_norm",
    )(x1, gates, g_final, y, y)


def _final_norm_kernel(x_ref, g_ref, o_ref):
    o_ref[...] = _rms(x_ref[...], g_ref[...])


def _final_norm(x, g):
    return pl.pallas_call(
        _final_norm_kernel,
        grid=(N_TOK // ROW_TILE,),
        in_specs=[pl.BlockSpec((ROW_TILE, D_MODEL), lambda i: (i, 0)),
                  pl.BlockSpec((1, D_MODEL), lambda i: (0, 0))],
        out_specs=pl.BlockSpec((ROW_TILE, D_MODEL), lambda i: (i, 0)),
        out_shape=jax.ShapeDtypeStruct((N_TOK, D_MODEL), F32),
        compiler_params=pltpu.CompilerParams(dimension_semantics=("parallel",)),
        name="final_norm",
    )(x, g)


def _expert_row_plan(route, counts):
    experts = route[_R_E1:_R_E2 + 1].T.astype(jnp.int32)
    ranks = route[_R_RANK1:_R_RANK2 + 1].T.astype(jnp.int32)
    counts = counts[:, 0].astype(jnp.int32)
    padded = ((counts + MOE_BLOCK - 1) // MOE_BLOCK) * MOE_BLOCK
    ends = jnp.cumsum(padded)
    starts = ends - padded
    start_of = jnp.sum(jnp.where(experts[..., None] == jnp.arange(N_EXPERTS), starts, 0), axis=-1)
    dest = (start_of + ranks).reshape(-1)
    block_row = jnp.arange(N_MOE_BLOCKS, dtype=jnp.int32) * MOE_BLOCK
    block_e = jnp.minimum(jnp.sum(block_row[:, None] >= ends[None, :], axis=-1), N_EXPERTS - 1)
    pad_lo = jnp.concatenate([starts + counts, ends[-1:]])
    pad_hi = jnp.concatenate([ends, jnp.full((1,), MOE_CAP, jnp.int32)])
    return dest.astype(jnp.int32), block_e.astype(jnp.int32), pad_lo.astype(jnp.int32), pad_hi.astype(jnp.int32)


def kernel(x, g_mix, w_in, sgu_ln_g, sgu_w, sgu_b, conv_w, w_out, g_ffn, dense_w_gate, dense_w_up,
           dense_w_down, router_w, moe_w_gate, moe_w_up, moe_w_down, g_final):
    x = x.reshape(N_TOK, D_MODEL)
    bias_table = _dilated_bias_table()
    out = None
    for l in range(DEPTH):
        za, zb, zc, zd = _norm_inproj(x, g_mix[l].reshape(1, D_MODEL), w_in[l])
        ya = _moba(za)
        yd = _dilated(zd, bias_table)
        yb, yc = _sgu_conv(zb, zc, sgu_ln_g[l], sgu_w[l], sgu_b[l], conv_w[l])
        ys = (ya, yb, yc, yd)
        g2 = g_ffn[l].reshape(1, D_MODEL)
        j = l // 2
        if l % 2 == 0:
            x1, h = _outproj(x, ys, w_out[l], g2)
            x = _dense_ffn(x1, h, dense_w_gate[j].astype(BF16), dense_w_up[j].astype(BF16),
                           dense_w_down[j].astype(BF16))
            if l == DEPTH - 1:
                out = _final_norm(x, g_final.reshape(1, D_MODEL))
        else:
            wr_hi, wr_lo = _split_bf16(router_w[j].T)
            router = (jnp.concatenate([wr_hi, wr_lo]), jnp.concatenate([wr_hi, jnp.zeros_like(wr_hi)]))
            x1, h, route, counts = _outproj(x, ys, w_out[l], g2, router=router)
            dest, block_e, pad_lo, pad_hi = _expert_row_plan(route, counts)
            src_row, dst_row = _invert(dest, pad_lo, pad_hi)
            y = _experts(block_e, src_row, dst_row, h, moe_w_gate[j].astype(BF16), moe_w_up[j].astype(BF16),
                         moe_w_down[j].astype(BF16))
            if l == DEPTH - 1:
                out = _combine_final_norm(x1, route[_R_G1:_R_G2 + 1].T, g_final.reshape(1, D_MODEL), y)
            else:
                raise NotImplementedError("an expert layer that is not the last layer")
    return out.reshape(BATCH, SEQ, D_MODEL)
```

```python
import functools

import numpy as np
import jax
import jax.numpy as jnp
from jax import lax
from jax.experimental import pallas as pl
from jax.experimental.pallas import tpu as pltpu

D_MODEL = 1024
BATCH = 8
SEQ = 2048
DEPTH = 2
N_TOK = BATCH * SEQ
HEAD_DIM = 64
N_HEADS = 4
W_MIX = N_HEADS * HEAD_DIM
CUT_A = 3 * W_MIX
CUT_B = CUT_A + 2 * W_MIX
CUT_C = CUT_B + 3 * W_MIX
IN_COLS = CUT_C + 3 * W_MIX
ATT_BLOCK = 256
HEAD_ROWS = HEAD_DIM + 16
N_ATT_BLOCKS = SEQ // ATT_BLOCK
MOBA_TOPK = 3
SGU_CHUNK = 128
DILATED_PATTERNS = ((128, 1), (512, 4), (2048, 16))
D_FF = 2816
FF_CHUNK = D_FF // 2
N_EXPERTS = 8
MOE_BLOCK = 256
N_MOE_BLOCKS = (N_TOK * 2) // MOE_BLOCK + N_EXPERTS
MOE_CAP = N_MOE_BLOCKS * MOE_BLOCK
Y_PLANE = N_TOK + 2 * MOE_BLOCK
RING = 3
RMS_EPS = 1e-6
LN_EPS = 1e-5
ATTN_SCALE = HEAD_DIM ** -0.5
MASK_BIAS = -1e30

LANES = 128
MXU_TILE = 256
ROW_TILE = 512
VMEM_LIMIT = 56 * 1024 * 1024

F32 = jnp.float32
BF16 = jnp.bfloat16


def _dot(a, b):
    return jnp.dot(a, b, preferred_element_type=F32)


def _dot_nt(a, b):
    return lax.dot_general(a, b, (((1,), (1,)), ((), ())), preferred_element_type=F32)


def _rms(x, g):
    return x * lax.rsqrt(jnp.mean(x * x, axis=-1, keepdims=True) + RMS_EPS) * g


def _cast_weight_once(w_ref, w16_ref):
    @pl.when(pl.program_id(0) == 0)
    def _():
        for c0 in range(0, w_ref.shape[1], MXU_TILE):
            w16_ref[:, c0:c0 + MXU_TILE] = w_ref[:, c0:c0 + MXU_TILE].astype(BF16)


def _split_bf16(x):
    hi = x.astype(BF16)
    lo = (x - hi.astype(F32)).astype(BF16)
    return hi, lo


def _norm_inproj_kernel(x_ref, g_ref, w32_ref, za_ref, zb_ref, zc_ref, zd_ref, w_ref):
    _cast_weight_once(w32_ref, w_ref)
    h = _rms(x_ref[...], g_ref[...]).astype(BF16)
    za_ref[...] = _dot(h, w_ref[:, 0:CUT_A]).astype(BF16)
    zb_ref[...] = _dot(h, w_ref[:, CUT_A:CUT_B]).astype(BF16)
    zc_ref[...] = _dot(h, w_ref[:, CUT_B:CUT_C]).astype(BF16)
    zd_ref[...] = _dot(h, w_ref[:, CUT_C:IN_COLS]).astype(BF16)


def _norm_inproj(x, g, w, layer):
    widths = (CUT_A, CUT_B - CUT_A, CUT_C - CUT_B, IN_COLS - CUT_C)
    return pl.pallas_call(
        _norm_inproj_kernel,
        grid=(N_TOK // ROW_TILE,),
        in_specs=[
            pl.BlockSpec((ROW_TILE, D_MODEL), lambda i: (i, 0)),
            pl.BlockSpec((1, D_MODEL), lambda i: (0, 0)),
            pl.BlockSpec((None, D_MODEL, IN_COLS), lambda i: (layer, 0, 0), pipeline_mode=pl.Buffered(1)),
        ],
        out_specs=[pl.BlockSpec((ROW_TILE, w_), lambda i: (i, 0)) for w_ in widths],
        out_shape=[jax.ShapeDtypeStruct((N_TOK, w_), BF16) for w_ in widths],
        scratch_shapes=[pltpu.VMEM((D_MODEL, IN_COLS), BF16)],
        compiler_params=pltpu.CompilerParams(
            dimension_semantics=("arbitrary",), vmem_limit_bytes=VMEM_LIMIT),
        name="norm_inproj",
    )(x, g, w)


def _head_lane_mask(h, width):
    lane = lax.broadcasted_iota(jnp.int32, (ATT_BLOCK, width), 1)
    return (lane >= h * HEAD_DIM) & (lane < (h + 1) * HEAD_DIM)


def _store_transposed_v(z_ref, vt_ref):
    row = lax.broadcasted_iota(jnp.int32, (HEAD_ROWS - HEAD_DIM, ATT_BLOCK), 0)
    ones_row = jnp.where(row == 0, 1.0, 0.0).astype(BF16)
    for j in range(N_ATT_BLOCKS):
        vt = z_ref[j * ATT_BLOCK:(j + 1) * ATT_BLOCK, 2 * W_MIX:3 * W_MIX].astype(F32).T
        for h in range(N_HEADS):
            vt_ref[j, h * HEAD_ROWS:h * HEAD_ROWS + HEAD_DIM, :] = (
                vt[h * HEAD_DIM:(h + 1) * HEAD_DIM].astype(BF16))
            vt_ref[j, h * HEAD_ROWS + HEAD_DIM:(h + 1) * HEAD_ROWS, :] = ones_row


def _stacked_queries(z_ref, row0):
    qs = z_ref[pl.ds(row0, ATT_BLOCK), 0:W_MIX].astype(F32) * ATTN_SCALE
    return jnp.concatenate(
        [jnp.where(_head_lane_mask(h, W_MIX), qs, 0.0) for h in range(N_HEADS)], axis=0).astype(BF16)


def _head_cols(h):
    return slice(h * ATT_BLOCK, (h + 1) * ATT_BLOCK)


def _values_times_probs(vt_ref, blk, p):
    return jnp.concatenate(
        [_dot(vt_ref[blk, h * HEAD_ROWS:(h + 1) * HEAD_ROWS, :], p[:, _head_cols(h)]) for h in range(N_HEADS)],
        axis=0)


def _per_head_rows(a):
    return jnp.concatenate(
        [jnp.broadcast_to(a[:, _head_cols(h)], (HEAD_ROWS, ATT_BLOCK)) for h in range(N_HEADS)], axis=0)


def _softmax_first(s, vt_ref, blk):
    m = jnp.max(s, axis=0, keepdims=True)
    return m, _values_times_probs(vt_ref, blk, jnp.exp(s - m).astype(BF16))


def _softmax_step(s, bias_row, vt_ref, blk, state):
    m, acc = state
    m_new = jnp.maximum(m, jnp.max(s, axis=0, keepdims=True) + bias_row)
    alpha = jnp.exp(m - m_new)
    p = jnp.exp(s - (m_new - bias_row))
    acc = _per_head_rows(alpha) * acc + _values_times_probs(vt_ref, blk, p.astype(BF16))
    return m_new, acc


def _attend(i, own_scores, scores, bias_row, vt_ref, s_refs, o_ref):
    s_a, s_b = s_refs
    last = jnp.maximum(i - 1, 0)
    s_a[...] = own_scores
    s_b[...] = scores(0)
    state = _softmax_first(s_a[...], vt_ref, i)
    s_a[...] = scores(jnp.minimum(1, last))
    state = _softmax_step(s_b[...], bias_row(0), vt_ref, 0, state)

    def two_blocks(t, state):
        j0 = 2 * t + 1
        j1 = j0 + 1
        s_b[...] = scores(jnp.minimum(j1, last))
        state = _softmax_step(s_a[...], bias_row(j0), vt_ref, j0, state)
        s_a[...] = scores(jnp.minimum(j1 + 1, last))
        return _softmax_step(s_b[...], bias_row(j1), vt_ref, j1, state)

    _, acc = lax.fori_loop(0, i // 2, two_blocks, state)
    heads = []
    for h in range(N_HEADS):
        r0 = h * HEAD_ROWS
        heads.append(acc[r0:r0 + HEAD_DIM] / acc[r0 + HEAD_DIM:r0 + HEAD_DIM + 1])
    o_ref[...] = jnp.concatenate(heads, axis=0).T.astype(o_ref.dtype)


def _moba_kernel(z_ref, o_ref, kmt_hi_ref, kmt_lo_ref, vt_ref, bias_ref, sa_ref, sb_ref):
    i = pl.program_id(1)

    @pl.when(i == 0)
    def _prepare_batch():
        means = []
        for j in range(N_ATT_BLOCKS):
            kj = z_ref[j * ATT_BLOCK:(j + 1) * ATT_BLOCK, W_MIX:2 * W_MIX].astype(F32)
            means.append(jnp.sum(kj, axis=0, keepdims=True) * (1.0 / ATT_BLOCK))
        mt = jnp.concatenate(means * N_HEADS, axis=0)
        r = lax.broadcasted_iota(jnp.int32, mt.shape, 0)
        c = lax.broadcasted_iota(jnp.int32, mt.shape, 1)
        hi, lo = _split_bf16(jnp.where((c >> 6) == (r >> 3), mt, 0.0))
        kmt_hi_ref[...] = hi
        kmt_lo_ref[...] = lo
        _store_transposed_v(z_ref, vt_ref)

    row0 = pl.multiple_of(i * ATT_BLOCK, ATT_BLOCK)
    q = z_ref[pl.ds(row0, ATT_BLOCK), 0:W_MIX]
    q_stack = _stacked_queries(z_ref, row0)

    def scores(blk):
        r0 = pl.multiple_of(blk * ATT_BLOCK, ATT_BLOCK)
        return _dot_nt(z_ref[pl.ds(r0, ATT_BLOCK), W_MIX:2 * W_MIX], q_stack)

    gates = _dot_nt(kmt_hi_ref[...], q) + _dot_nt(kmt_lo_ref[...], q)
    blk = lax.broadcasted_iota(jnp.int32, (N_ATT_BLOCKS, ATT_BLOCK), 0)
    past = blk < i
    biases = []
    for h in range(N_HEADS):
        g = jnp.where(past, gates[h * N_ATT_BLOCKS:(h + 1) * N_ATT_BLOCKS], -jnp.inf)
        rank = jnp.zeros(g.shape, F32)
        for d in range(1, N_ATT_BLOCKS):
            lower = pltpu.roll(g, d, axis=0)
            rank = rank + jnp.where((blk >= d) & (lower >= g), 1.0, 0.0)
            upper = pltpu.roll(g, N_ATT_BLOCKS - d, axis=0)
            rank = rank + jnp.where((blk + d < N_ATT_BLOCKS) & (upper > g), 1.0, 0.0)
        biases.append(jnp.where(past & (rank < MOBA_TOPK), 0.0, MASK_BIAS))
    bias = jnp.concatenate(biases, axis=1)
    for j in range(N_ATT_BLOCKS):
        bias_ref[j] = bias[j:j + 1]

    ki = lax.broadcasted_iota(jnp.int32, (ATT_BLOCK, N_HEADS * ATT_BLOCK), 0)
    qi = lax.broadcasted_iota(jnp.int32, (ATT_BLOCK, N_HEADS * ATT_BLOCK), 1) & (ATT_BLOCK - 1)
    own = jnp.where(ki <= qi, scores(i), -jnp.inf)
    _attend(i, own, scores, lambda j: bias_ref[j], vt_ref, (sa_ref, sb_ref), o_ref)


def _moba(za):
    za3 = za.reshape(BATCH, SEQ, CUT_A)
    out = pl.pallas_call(
        _moba_kernel,
        grid=(BATCH, N_ATT_BLOCKS),
        in_specs=[pl.BlockSpec((None, SEQ, CUT_A), lambda b, i: (b, 0, 0))],
        out_specs=pl.BlockSpec((None, ATT_BLOCK, W_MIX), lambda b, i: (b, i, 0)),
        out_shape=jax.ShapeDtypeStruct((BATCH, SEQ, W_MIX), BF16),
        scratch_shapes=[
            pltpu.VMEM((N_HEADS * N_ATT_BLOCKS, W_MIX), BF16),
            pltpu.VMEM((N_HEADS * N_ATT_BLOCKS, W_MIX), BF16),
            pltpu.VMEM((N_ATT_BLOCKS, N_HEADS * HEAD_ROWS, ATT_BLOCK), BF16),
            pltpu.VMEM((N_ATT_BLOCKS, 1, N_HEADS * ATT_BLOCK), F32),
            pltpu.VMEM((ATT_BLOCK, N_HEADS * ATT_BLOCK), F32),
            pltpu.VMEM((ATT_BLOCK, N_HEADS * ATT_BLOCK), F32),
        ],
        compiler_params=pltpu.CompilerParams(
            dimension_semantics=("parallel", "arbitrary"), vmem_limit_bytes=VMEM_LIMIT),
        name="moba_attention",
    )(za3)
    return out.reshape(N_TOK, W_MIX)


def _dilated_bias_table():
    d = np.arange(ATT_BLOCK)[None, :] - np.arange(ATT_BLOCK)[:, None]
    tiles = []
    for k in range(N_ATT_BLOCKS):
        dist = d + k * ATT_BLOCK
        count = np.zeros(dist.shape, np.float64)
        for window, dil in DILATED_PATTERNS:
            count += (dist >= 0) & (dist <= window) & (dist % dil == 0)
        with np.errstate(divide="ignore"):
            tiles.append(np.log(count))
    return jnp.asarray(np.stack(tiles), F32)


def _dilated_kernel(z_ref, bias_ref, o_ref, vt_ref, sa_ref, sb_ref):
    i = pl.program_id(1)

    @pl.when(i == 0)
    def _prepare_batch():
        _store_transposed_v(z_ref, vt_ref)

    row0 = pl.multiple_of(i * ATT_BLOCK, ATT_BLOCK)
    q_stack = _stacked_queries(z_ref, row0)

    def scores(blk):
        r0 = pl.multiple_of(blk * ATT_BLOCK, ATT_BLOCK)
        s = _dot_nt(z_ref[pl.ds(r0, ATT_BLOCK), W_MIX:2 * W_MIX], q_stack)
        b = bias_ref[i - blk]
        return jnp.concatenate([s[:, _head_cols(h)] + b for h in range(N_HEADS)], axis=1)

    def bias_row(j):
        return jnp.full((1, N_HEADS * ATT_BLOCK), jnp.where(j < i, 0.0, MASK_BIAS), F32)

    _attend(i, scores(i), scores, bias_row, vt_ref, (sa_ref, sb_ref), o_ref)


def _dilated(zd, bias_table):
    zd3 = zd.reshape(BATCH, SEQ, 3 * W_MIX)
    out = pl.pallas_call(
        _dilated_kernel,
        grid=(BATCH, N_ATT_BLOCKS),
        in_specs=[
            pl.BlockSpec((None, SEQ, 3 * W_MIX), lambda b, i: (b, 0, 0)),
            pl.BlockSpec((N_ATT_BLOCKS, ATT_BLOCK, ATT_BLOCK), lambda b, i: (0, 0, 0)),
        ],
        out_specs=pl.BlockSpec((None, ATT_BLOCK, W_MIX), lambda b, i: (b, i, 0)),
        out_shape=jax.ShapeDtypeStruct((BATCH, SEQ, W_MIX), BF16),
        scratch_shapes=[pltpu.VMEM((N_ATT_BLOCKS, N_HEADS * HEAD_ROWS, ATT_BLOCK), BF16),
                        pltpu.VMEM((ATT_BLOCK, N_HEADS * ATT_BLOCK), F32),
                        pltpu.VMEM((ATT_BLOCK, N_HEADS * ATT_BLOCK), F32)],
        compiler_params=pltpu.CompilerParams(
            dimension_semantics=("parallel", "arbitrary"), vmem_limit_bytes=VMEM_LIMIT),
        name="dilated_attention",
    )(zd3, bias_table)
    return out.reshape(N_TOK, W_MIX)


def _gelu_tanh(x):
    return 0.5 * x * (1.0 + jnp.tanh(np.sqrt(2.0 / np.pi).astype(np.float32) * (x + 0.044715 * (x * x * x))))


def _sgu_conv_kernel(zb_ref, zc_ref, lng_ref, ws_ref, bs_ref, cw_ref, yb_ref, yc_ref):
    r = lax.broadcasted_iota(jnp.int32, (N_HEADS * SGU_CHUNK, SGU_CHUNK), 0)
    c = lax.broadcasted_iota(jnp.int32, (N_HEADS * SGU_CHUNK, SGU_CHUNK), 1)
    w_stack = jnp.where((r & (SGU_CHUNK - 1)) >= c, ws_ref[...], 0.0).astype(BF16)
    group = lax.broadcasted_iota(jnp.int32, (SGU_CHUNK, W_MIX), 1) >> 6
    ln_g = lng_ref[...]
    b_tile = bs_ref[...]

    def chunk(n, carry):
        r0 = pl.multiple_of(n * SGU_CHUNK, SGU_CHUNK)
        gz = _gelu_tanh(zb_ref[pl.ds(r0, SGU_CHUNK), :].astype(F32))
        u = gz[:, 0:W_MIX]
        v = gz[:, W_MIX:2 * W_MIX]
        vc = v - jnp.mean(v, axis=-1, keepdims=True)
        vn = vc * lax.rsqrt(jnp.mean(vc * vc, axis=-1, keepdims=True) + LN_EPS) * ln_g
        mixed = _dot(w_stack, vn.astype(BF16))
        sv = mixed[0:SGU_CHUNK]
        for gi in range(1, N_HEADS):
            sv = jnp.where(group == gi, mixed[gi * SGU_CHUNK:(gi + 1) * SGU_CHUNK], sv)
        yb_ref[pl.ds(r0, SGU_CHUNK), :] = (u * (sv + b_tile)).astype(yb_ref.dtype)
        return carry

    lax.fori_loop(0, SEQ // SGU_CHUNK, chunk, 0, unroll=4)

    zc = zc_ref[...].astype(F32)
    z = zc[:, W_MIX:2 * W_MIX] * zc[:, 2 * W_MIX:3 * W_MIX]
    t = lax.broadcasted_iota(jnp.int32, (SEQ, W_MIX), 0)
    z1 = jnp.where(t >= 1, pltpu.roll(z, 1, axis=0), 0.0)
    z2 = jnp.where(t >= 2, pltpu.roll(z, 2, axis=0), 0.0)
    y = z2 * cw_ref[0:1, :]
    y = y + z1 * cw_ref[1:2, :]
    y = y + z * cw_ref[2:3, :]
    yc_ref[...] = (zc[:, 0:W_MIX] * y).astype(yc_ref.dtype)


def _sgu_conv(zb, zc, ln_g, w_s, b_s, conv_w):
    w_stack = w_s.reshape(N_HEADS * SGU_CHUNK, SGU_CHUNK)
    b_tile = jnp.repeat(b_s.T, HEAD_DIM, axis=1)
    yb, yc = pl.pallas_call(
        _sgu_conv_kernel,
        grid=(BATCH,),
        in_specs=[
            pl.BlockSpec((None, SEQ, 2 * W_MIX), lambda b: (b, 0, 0)),
            pl.BlockSpec((None, SEQ, 3 * W_MIX), lambda b: (b, 0, 0)),
            pl.BlockSpec((1, W_MIX), lambda b: (0, 0)),
            pl.BlockSpec((N_HEADS * SGU_CHUNK, SGU_CHUNK), lambda b: (0, 0)),
            pl.BlockSpec((SGU_CHUNK, W_MIX), lambda b: (0, 0)),
            pl.BlockSpec((3, W_MIX), lambda b: (0, 0)),
        ],
        out_specs=[pl.BlockSpec((None, SEQ, W_MIX), lambda b: (b, 0, 0))] * 2,
        out_shape=[jax.ShapeDtypeStruct((BATCH, SEQ, W_MIX), BF16)] * 2,
        compiler_params=pltpu.CompilerParams(
            dimension_semantics=("parallel",), vmem_limit_bytes=VMEM_LIMIT),
        name="sgu_conv",
    )(zb.reshape(BATCH, SEQ, 2 * W_MIX), zc.reshape(BATCH, SEQ, 3 * W_MIX),
      ln_g.reshape(1, W_MIX), w_stack, b_tile, conv_w)
    return yb.reshape(N_TOK, W_MIX), yc.reshape(N_TOK, W_MIX)


def _outproj_residual(x_ref, ya_ref, yb_ref, yc_ref, yd_ref, w_ref):
    acc = _dot(ya_ref[...], w_ref[0:W_MIX, :])
    acc = acc + _dot(yb_ref[...], w_ref[W_MIX:2 * W_MIX, :])
    acc = acc + _dot(yc_ref[...], w_ref[2 * W_MIX:3 * W_MIX, :])
    acc = acc + _dot(yd_ref[...], w_ref[3 * W_MIX:4 * W_MIX, :])
    return x_ref[...] + acc


def _outproj_dense_kernel(x_ref, ya_ref, yb_ref, yc_ref, yd_ref, w32_ref, g_ref, x1_ref, h_ref, w_ref):
    _cast_weight_once(w32_ref, w_ref)
    x1 = _outproj_residual(x_ref, ya_ref, yb_ref, yc_ref, yd_ref, w_ref)
    x1_ref[...] = x1
    h_ref[...] = _rms(x1, g_ref[...]).astype(BF16)


_R_E1, _R_E2, _R_G1, _R_G2, _R_RANK1, _R_RANK2 = range(6)


def _outproj_router_kernel(x_ref, ya_ref, yb_ref, yc_ref, yd_ref, w32_ref, g_ref, wra_ref, wrb_ref,
                           x1_ref, h_ref, route_ref, counts_ref, running_ref, w_ref):
    step = pl.program_id(0)
    _cast_weight_once(w32_ref, w_ref)

    @pl.when(step == 0)
    def _():
        running_ref[...] = jnp.zeros_like(running_ref)

    x1 = _outproj_residual(x_ref, ya_ref, yb_ref, yc_ref, yd_ref, w_ref)
    x1_ref[...] = x1
    h = _rms(x1, g_ref[...])
    h_ref[...] = h

    h_hi, h_lo = _split_bf16(h)
    part = _dot_nt(wra_ref[...], h_hi)
    logits = part[0:N_EXPERTS] + (part[N_EXPERTS:2 * N_EXPERTS] + _dot_nt(wrb_ref[...], h_lo)[0:N_EXPERTS])
    expert = lax.broadcasted_iota(jnp.int32, (N_EXPERTS, ROW_TILE), 0).astype(F32)
    m1 = jnp.max(logits, axis=0, keepdims=True)
    e1 = jnp.min(jnp.where(logits == m1, expert, float(N_EXPERTS)), axis=0, keepdims=True)
    rest = jnp.where(expert == e1, -jnp.inf, logits)
    m2 = jnp.max(rest, axis=0, keepdims=True)
    e2 = jnp.min(jnp.where(rest == m2, expert, float(N_EXPERTS)), axis=0, keepdims=True)
    t = jnp.exp(m2 - m1)
    g1 = 1.0 / (1.0 + t)
    g2 = t / (1.0 + t)

    chosen = jnp.where((expert == e1) | (expert == e2), 1.0, 0.0)
    ri = lax.broadcasted_iota(jnp.int32, (ROW_TILE, ROW_TILE), 0)
    ci = lax.broadcasted_iota(jnp.int32, (ROW_TILE, ROW_TILE), 1)
    earlier = jnp.where(ri < ci, 1.0, 0.0).astype(BF16)
    chosen16 = jnp.concatenate([chosen, jnp.zeros_like(chosen)], axis=0).astype(BF16)
    before = _dot(chosen16, earlier)[0:N_EXPERTS] + running_ref[:, 0:1]
    rank1 = jnp.sum(jnp.where(expert == e1, before, 0.0), axis=0, keepdims=True)
    rank2 = jnp.sum(jnp.where(expert == e2, before, 0.0), axis=0, keepdims=True)
    running_ref[...] = running_ref[...] + jnp.sum(chosen, axis=1, keepdims=True)
    counts_ref[...] = running_ref[...]

    rec = jnp.zeros((N_EXPERTS, ROW_TILE), F32)
    for pos, val in ((_R_E1, e1), (_R_E2, e2), (_R_G1, g1), (_R_G2, g2),
                     (_R_RANK1, rank1), (_R_RANK2, rank2)):
        rec = jnp.where(expert == float(pos), val, rec)
    route_ref[...] = rec


def _outproj(x, ys, w, g, layer, router=None):
    row = lambda width: pl.BlockSpec((ROW_TILE, width), lambda i: (i, 0))
    const = lambda shape: pl.BlockSpec(shape, lambda i: (0, 0))
    weight = pl.BlockSpec((None, D_MODEL, D_MODEL), lambda i: (layer, 0, 0), pipeline_mode=pl.Buffered(1))
    w16 = pltpu.VMEM((D_MODEL, D_MODEL), BF16)
    in_specs = [row(D_MODEL)] + [row(W_MIX)] * 4 + [weight, const((1, D_MODEL))]
    if router is None:
        return pl.pallas_call(
            _outproj_dense_kernel,
            grid=(N_TOK // ROW_TILE,),
            in_specs=in_specs,
            out_specs=[row(D_MODEL), row(D_MODEL)],
            out_shape=[jax.ShapeDtypeStruct((N_TOK, D_MODEL), F32),
                       jax.ShapeDtypeStruct((N_TOK, D_MODEL), BF16)],
            scratch_shapes=[w16],
            compiler_params=pltpu.CompilerParams(
                dimension_semantics=("arbitrary",), vmem_limit_bytes=VMEM_LIMIT),
            name="outproj_norm",
        )(x, *ys, w, g)
    wra, wrb = router
    return pl.pallas_call(
        _outproj_router_kernel,
        grid=(N_TOK // ROW_TILE,),
        in_specs=in_specs + [const((2 * N_EXPERTS, D_MODEL)), const((2 * N_EXPERTS, D_MODEL))],
        out_specs=[row(D_MODEL), row(D_MODEL),
                   pl.BlockSpec((N_EXPERTS, ROW_TILE), lambda i: (0, i)), const((N_EXPERTS, LANES))],
        out_shape=[jax.ShapeDtypeStruct((N_TOK, D_MODEL), F32),
                   jax.ShapeDtypeStruct((N_TOK, D_MODEL), F32),
                   jax.ShapeDtypeStruct((N_EXPERTS, N_TOK), F32),
                   jax.ShapeDtypeStruct((N_EXPERTS, LANES), F32)],
        scratch_shapes=[pltpu.VMEM((N_EXPERTS, LANES), F32), w16],
        compiler_params=pltpu.CompilerParams(
            dimension_semantics=("arbitrary",), vmem_limit_bytes=VMEM_LIMIT),
        name="outproj_router",
    )(x, *ys, w, g, wra, wrb)


def _silu_gate(gate, up):
    return (gate * (1.0 / (1.0 + jnp.exp(-gate))) * up).astype(BF16)


def _swiglu_rows(h, wg_ref, wu_ref, wd_ref):
    acc = None
    for c0 in range(0, D_FF, FF_CHUNK):
        act = _silu_gate(_dot(h, wg_ref[:, c0:c0 + FF_CHUNK]), _dot(h, wu_ref[:, c0:c0 + FF_CHUNK]))
        part = _dot(act, wd_ref[c0:c0 + FF_CHUNK, :])
        acc = part if acc is None else acc + part
    return acc


def _swiglu_rows_spreading(h, wg_ref, wu_ref, wd_ref, copies):
    copies = list(copies)
    n_gaps = 2 * (D_FF // MXU_TILE) + D_MODEL // MXU_TILE
    per_gap = -(-len(copies) // n_gaps)

    def start_some():
        for c in copies[:per_gap]:
            c.start()
        del copies[:per_gap]

    acts = []
    for c0 in range(0, D_FF, MXU_TILE):
        gate = _dot(h, wg_ref[:, c0:c0 + MXU_TILE])
        start_some()
        up = _dot(h, wu_ref[:, c0:c0 + MXU_TILE])
        start_some()
        acts.append(_silu_gate(gate, up))
    act = jnp.concatenate(acts, axis=1)
    outs = []
    for n0 in range(0, D_MODEL, MXU_TILE):
        outs.append(_dot(act, wd_ref[:, n0:n0 + MXU_TILE]))
        start_some()
    return jnp.concatenate(outs, axis=1)


def _dense_ffn_kernel(x_ref, h_ref, wg_ref, wu_ref, wd_ref, o_ref):
    o_ref[...] = x_ref[...] + _swiglu_rows(h_ref[...], wg_ref, wu_ref, wd_ref)


def _dense_ffn(x1, h, wg, wu, wd):
    row = lambda: pl.BlockSpec((ROW_TILE, D_MODEL), lambda i: (i, 0))
    once = pl.Buffered(1)
    return pl.pallas_call(
        _dense_ffn_kernel,
        grid=(N_TOK // ROW_TILE,),
        in_specs=[
            row(), row(),
            pl.BlockSpec((D_MODEL, D_FF), lambda i: (0, 0), pipeline_mode=once),
            pl.BlockSpec((D_MODEL, D_FF), lambda i: (0, 0), pipeline_mode=once),
            pl.BlockSpec((D_FF, D_MODEL), lambda i: (0, 0), pipeline_mode=once),
        ],
        out_specs=row(),
        out_shape=jax.ShapeDtypeStruct((N_TOK, D_MODEL), F32),
        compiler_params=pltpu.CompilerParams(
            dimension_semantics=("parallel",), vmem_limit_bytes=VMEM_LIMIT),
        name="dense_swiglu",
    )(x1, h, wg, wu, wd)


def _row_copy(src_ref, src_row, dst_ref, dst_row, sem):
    return pltpu.make_async_copy(src_ref.at[pl.ds(src_row, 1), :], dst_ref.at[pl.ds(dst_row, 1), :], sem)


def _invert_kernel(dest_ref, pad_lo_ref, pad_hi_ref, src_row_ref, dst_row_ref):
    for e in range(N_EXPERTS + 1):
        def mark_pad(row, carry):
            src_row_ref[row] = 0
            dst_row_ref[row] = N_TOK + (row & (2 * MOE_BLOCK - 1))
            return carry

        lax.fori_loop(pad_lo_ref[e], pad_hi_ref[e], mark_pad, 0)

    def place(token, carry):
        row1 = dest_ref[2 * token]
        row2 = dest_ref[2 * token + 1]
        src_row_ref[row1] = token
        src_row_ref[row2] = token
        dst_row_ref[row1] = token
        dst_row_ref[row2] = Y_PLANE + token
        return carry

    lax.fori_loop(0, N_TOK, place, 0, unroll=8)


def _invert(dest_flat, pad_lo, pad_hi):
    return pl.pallas_call(
        _invert_kernel,
        grid_spec=pltpu.PrefetchScalarGridSpec(
            num_scalar_prefetch=3,
            grid=(1,),
            in_specs=[],
            out_specs=[pl.BlockSpec(memory_space=pltpu.SMEM)] * 2,
        ),
        out_shape=[jax.ShapeDtypeStruct((MOE_CAP,), jnp.int32)] * 2,
        name="moe_invert",
    )(dest_flat, pad_lo, pad_hi)


def _expert_kernel(block_e_ref, src_row_ref, dst_row_ref, h_ref, wg_ref, wu_ref, wd_ref, y_ref,
                   x_buf, y_buf, gather_sem, scatter_sem):
    del block_e_ref
    i = pl.program_id(0)
    last = N_MOE_BLOCKS - 1
    cur = lax.rem(i, RING)
    other = lax.rem(i + 2, RING)

    def gather(block, buf):
        return [_row_copy(h_ref, src_row_ref[block * MOE_BLOCK + r], x_buf.at[buf], r, gather_sem.at[buf])
                for r in range(MOE_BLOCK)]

    def scatter(block, buf, to_dump=None):
        copies = []
        for r in range(MOE_BLOCK):
            row = dst_row_ref[block * MOE_BLOCK + r]
            if to_dump is not None:
                row = jnp.where(to_dump, N_TOK + MOE_BLOCK + r, row)
            copies.append(_row_copy(y_buf.at[buf], r, y_ref, row, scatter_sem.at[buf]))
        return copies

    @pl.when(i == 0)
    def _():
        y_buf[...] = jnp.zeros_like(y_buf)
        for b in range(2):
            for c in gather(b, b):
                c.start()
            dump = pltpu.make_async_copy(
                y_buf.at[b], y_ref.at[pl.ds(N_TOK + b * MOE_BLOCK, MOE_BLOCK), :], scatter_sem.at[b])
            dump.start()
            dump.wait()

    for c in gather(i, cur):
        c.wait()

    @pl.when(i >= 2)
    def _():
        for c in scatter(jnp.maximum(i - 3, 0), cur):
            c.wait()

    copies = (gather(jnp.minimum(i + 2, last), other)
              + scatter(jnp.maximum(i - 1, 0), other, to_dump=(i == 0)))
    y_buf[cur] = _swiglu_rows_spreading(x_buf[cur].astype(BF16), wg_ref, wu_ref, wd_ref, copies)

    @pl.when(i == last)
    def _():
        for c in scatter(last - 2, (last - 2) % RING):
            c.wait()
        for c in scatter(last, last % RING):
            c.start()
        for extra in (last + 1, last + 2):
            for c in gather(last, extra % RING):
                c.wait()
        for block in (last - 1, last):
            for c in scatter(block, block % RING):
                c.wait()


def _experts(block_e, src_row, dst_row, h, wg, wu, wd):
    weights = lambda shape: pl.BlockSpec((None,) + shape, lambda i, be, sr, dr: (be[i], 0, 0))
    return pl.pallas_call(
        _expert_kernel,
        grid_spec=pltpu.PrefetchScalarGridSpec(
            num_scalar_prefetch=3,
            grid=(N_MOE_BLOCKS,),
            in_specs=[pl.BlockSpec(memory_space=pl.ANY),
                      weights((D_MODEL, D_FF)), weights((D_MODEL, D_FF)), weights((D_FF, D_MODEL))],
            out_specs=pl.BlockSpec(memory_space=pl.ANY),
            scratch_shapes=[pltpu.VMEM((RING, MOE_BLOCK, D_MODEL), F32),
                            pltpu.VMEM((RING, MOE_BLOCK, D_MODEL), F32),
                            pltpu.SemaphoreType.DMA((RING,)),
                            pltpu.SemaphoreType.DMA((RING,))],
        ),
        out_shape=jax.ShapeDtypeStruct((Y_PLANE + N_TOK, D_MODEL), F32),
        compiler_params=pltpu.CompilerParams(
            dimension_semantics=("arbitrary",), vmem_limit_bytes=VMEM_LIMIT),
        name="moe_experts",
    )(block_e, src_row, dst_row, h, wg, wu, wd)


def _combine_kernel(x_ref, gates_ref, g_ref, y1_ref, y2_ref, o_ref):
    gates = gates_ref[...]
    g1 = gates[:, 0:1]
    g2 = gates[:, 1:2]
    x2 = x_ref[...] + (y1_ref[...] * g1 + y2_ref[...] * g2)
    o_ref[...] = _rms(x2, g_ref[...])


def _combine_final_norm(x1, gates, g_final, y):
    row = lambda width: pl.BlockSpec((ROW_TILE, width), lambda i: (i, 0))
    return pl.pallas_call(
        _combine_kernel,
        grid=(N_TOK // ROW_TILE,),
        in_specs=[row(D_MODEL), row(2), pl.BlockSpec((1, D_MODEL), lambda i: (0, 0)),
                  row(D_MODEL),
                  pl.BlockSpec((ROW_TILE, D_MODEL), lambda i: (Y_PLANE // ROW_TILE + i, 0))],
        out_specs=row(D_MODEL),
        out_shape=jax.ShapeDtypeStruct((N_TOK, D_MODEL), F32),
        compiler_params=pltpu.CompilerParams(
            dimension_semantics=("parallel",), vmem_limit_bytes=VMEM_LIMIT),
        name="moe_combine_norm",
    )(x1, gates, g_final, y, y)


def _final_norm_kernel(x_ref, g_ref, o_ref):
    o_ref[...] = _rms(x_ref[...], g_ref[...])


def _final_norm(x, g):
    return pl.pallas_call(
        _final_norm_kernel,
        grid=(N_TOK // ROW_TILE,),
        in_specs=[pl.BlockSpec((ROW_TILE, D_MODEL), lambda i: (i, 0)),
                  pl.BlockSpec((1, D_MODEL), lambda i: (0, 0))],
        out_specs=pl.BlockSpec((ROW_TILE, D_MODEL), lambda i: (i, 0)),
        out_shape=jax.ShapeDtypeStruct((N_TOK, D_MODEL), F32),
        compiler_params=pltpu.CompilerParams(dimension_semantics=("parallel",)),
        name="final_norm",
    )(x, g)


def _expert_row_plan(route, counts):
    experts = route[_R_E1:_R_E2 + 1].T.astype(jnp.int32)
    ranks = route[_R_RANK1:_R_RANK2 + 1].T.astype(jnp.int32)
    counts = counts[:, 0].astype(jnp.int32)
    padded = ((counts + MOE_BLOCK - 1) // MOE_BLOCK) * MOE_BLOCK
    ends = jnp.cumsum(padded)
    starts = ends - padded
    start_of = jnp.sum(jnp.where(experts[..., None] == jnp.arange(N_EXPERTS), starts, 0), axis=-1)
    dest = (start_of + ranks).reshape(-1)
    block_row = jnp.arange(N_MOE_BLOCKS, dtype=jnp.int32) * MOE_BLOCK
    block_e = jnp.minimum(jnp.sum(block_row[:, None] >= ends[None, :], axis=-1), N_EXPERTS - 1)
    pad_lo = jnp.concatenate([starts + counts, ends[-1:]])
    pad_hi = jnp.concatenate([ends, jnp.full((1,), MOE_CAP, jnp.int32)])
    return dest.astype(jnp.int32), block_e.astype(jnp.int32), pad_lo.astype(jnp.int32), pad_hi.astype(jnp.int32)


def kernel(x, g_mix, w_in, sgu_ln_g, sgu_w, sgu_b, conv_w, w_out, g_ffn, dense_w_gate, dense_w_up,
           dense_w_down, router_w, moe_w_gate, moe_w_up, moe_w_down, g_final):
    x = x.reshape(N_TOK, D_MODEL)
    bias_table = _dilated_bias_table()
    out = None
    for l in range(DEPTH):
        za, zb, zc, zd = _norm_inproj(x, g_mix[l].reshape(1, D_MODEL), w_in, l)
        ya = _moba(za)
        yd = _dilated(zd, bias_table)
        yb, yc = _sgu_conv(zb, zc, sgu_ln_g[l], sgu_w[l], sgu_b[l], conv_w[l])
        ys = (ya, yb, yc, yd)
        g2 = g_ffn[l].reshape(1, D_MODEL)
        j = l // 2
        if l % 2 == 0:
            x1, h = _outproj(x, ys, w_out, g2, l)
            x = _dense_ffn(x1, h, dense_w_gate[j].astype(BF16), dense_w_up[j].astype(BF16),
                           dense_w_down[j].astype(BF16))
            if l == DEPTH - 1:
                out = _final_norm(x, g_final.reshape(1, D_MODEL))
        else:
            wr_hi, wr_lo = _split_bf16(router_w[j].T)
            router = (jnp.concatenate([wr_hi, wr_lo]), jnp.concatenate([wr_hi, jnp.zeros_like(wr_hi)]))
            x1, h, route, counts = _outproj(x, ys, w_out, g2, l, router=router)
            dest, block_e, pad_lo, pad_hi = _expert_row_plan(route, counts)
            src_row, dst_row = _invert(dest, pad_lo, pad_hi)
            y = _experts(block_e, src_row, dst_row, h, moe_w_gate[j].astype(BF16), moe_w_up[j].astype(BF16),
                         moe_w_down[j].astype(BF16))
            if l == DEPTH - 1:
                out = _combine_final_norm(x1, route[_R_G1:_R_G2 + 1].T, g_final.reshape(1, D_MODEL), y)
            else:
                raise NotImplementedError("an expert layer that is not the last layer")
    return out.reshape(BATCH, SEQ, D_MODEL)
```
